```python
import math, functools
import jax, jax.numpy as jnp
from jax import lax
import numpy as np


D_MODEL = 1024
BATCH = 16
SEQ = 2048
DEPTH = 2
DEC_BATCH = 32
DEC_SEQ = 1
PAST_LEN = 16384
PAGE_SIZE = 128

HEAD_DIM = 64
H_FOX = 6
FOX_W = H_FOX * HEAD_DIM
H_DIFF = 4
D_DIFF = HEAD_DIM // 2
DIFF_W = H_DIFF * HEAD_DIM
CONV_CH = D_MODEL - FOX_W - DIFF_W
CONV_WIDTH = 31
D_FF = ((8 * D_MODEL + 3 * 256 - 1) // (3 * 256)) * 256
Q_BLOCK = 128
DEEPNORM_ALPHA = (2.0 * DEPTH) ** 0.25
DEEPNORM_BETA = (8.0 * DEPTH) ** -0.25
LN_EPS = 1e-5
NEG_INF = -1e30
N_IN = 3 * FOX_W + H_FOX + 3 * DIFF_W + 2 * CONV_CH

kernel_name = "hymba_fox_diff_conformer_decoder_step"


def _layernorm(x, g, b):
    xf = x.astype(jnp.float32)
    mu = jnp.mean(xf, axis=-1, keepdims=True)
    var = jnp.mean(jnp.square(xf - mu), axis=-1, keepdims=True)
    return ((xf - mu) * lax.rsqrt(var + LN_EPS) * g + b).astype(x.dtype)


def _rmsnorm(x, g):
    xf = x.astype(jnp.float32)
    return xf * lax.rsqrt(jnp.mean(xf * xf, axis=-1, keepdims=True) + LN_EPS) * g


def _fox_attend(q, k, v, cq, ck, qpos, kpos):
    s = jnp.einsum('bqhd,bkhd->bhqk', q.astype(jnp.float32), k.astype(jnp.float32)) * (HEAD_DIM ** -0.5)
    s = s + jnp.swapaxes(cq, 1, 2)[:, :, :, None] - jnp.swapaxes(ck, 1, 2)[:, :, None, :]
    s = jnp.where(kpos[None, :] <= qpos[:, None], s, NEG_INF)
    p = jax.nn.softmax(s, axis=-1)
    return jnp.einsum('bhqk,bkhd->bqhd', p, v.astype(jnp.float32))


def _diff_attend(q, k, v, qpos, kpos):
    s = jnp.einsum('bqhpd,bkhpd->bhpqk', q.astype(jnp.float32), k.astype(jnp.float32)) * (D_DIFF ** -0.5)
    s = jnp.where(kpos[None, :] <= qpos[:, None], s, NEG_INF)
    p = jax.nn.softmax(s, axis=-1)
    return jnp.einsum('bhpqk,bkhe->bqhpe', p, v.astype(jnp.float32))


def _prompt_attend(fq, fk, fv, logf, dq, dk, dv):
    B, S = fq.shape[:2]
    nb = S // Q_BLOCK
    cum = jnp.cumsum(logf, axis=1)
    pos = jnp.arange(S)

    def to_blocks(a):
        return jnp.moveaxis(a.reshape((B, nb, Q_BLOCK) + a.shape[2:]), 1, 0)

    def from_blocks(a):
        return jnp.moveaxis(a, 0, 1).reshape((B, S) + a.shape[3:])

    def block(xs):
        fq_b, cq_b, dq_b, qpos = xs
        return (_fox_attend(fq_b, fk, fv, cq_b, cum, qpos, pos),
                _diff_attend(dq_b, dk, dv, qpos, pos))

    fo, do = lax.map(block, (to_blocks(fq), to_blocks(cum), to_blocks(dq), pos.reshape(nb, Q_BLOCK)))
    return from_blocks(fo), from_blocks(do)


def _sample_attend(kp, vp, lfp, dkp, dvp, fq, fk, fv, logf, dq, dk, dv):
    B, T = fq.shape[:2]
    P = kp.shape[1]
    k_all = jnp.concatenate([kp, fk.astype(kp.dtype)], axis=1)
    v_all = jnp.concatenate([vp, fv.astype(vp.dtype)], axis=1)
    cum = jnp.cumsum(jnp.concatenate([lfp.astype(jnp.float32), logf], axis=1), axis=1)
    dk_all = jnp.concatenate([dkp.reshape(B, P, H_DIFF, 2, D_DIFF), dk.astype(dkp.dtype)], axis=1)
    dv_all = jnp.concatenate([dvp, dv.astype(dvp.dtype)], axis=1)
    kpos = jnp.arange(P + T)
    qpos = P + jnp.arange(T)
    return (_fox_attend(fq, k_all, v_all, cum[:, P:], cum, qpos, kpos),
            _diff_attend(dq, dk_all, dv_all, qpos, kpos))


def _conformer_conv(val, gate, buf, conv_w, conv_b, g, b):
    u = val * jax.nn.sigmoid(gate)
    full = jnp.concatenate([buf.astype(u.dtype), u], axis=1)
    y = lax.conv_general_dilated(full, conv_w[:, None, :].astype(u.dtype), window_strides=(1,),
                                 padding='VALID', dimension_numbers=('NWC', 'WIO', 'NWC'),
                                 feature_group_count=CONV_CH) + conv_b
    y = jax.nn.silu(_layernorm(y, g, b))
    return y, full[:, -(CONV_WIDTH - 1):]


def _layer(l, x, c, conv_buf, attend, w_ada, b_ada, w_in, b_fgate, lq1, lk1, lq2, lk2, diff_g,
           conv_w, conv_b, conv_ln_g, conv_ln_b, w_o, ln_mix_g, ln_mix_b,
           w_ffn_in, w_ffn_out, ln_ffn_g, ln_ffn_b):
    B, T, _ = x.shape
    ada = jnp.einsum('bd,de->be', jax.nn.silu(c), w_ada) + b_ada
    sh_a, sc_a, g_a, sh_f, sc_f, g_f = [a[:, None, :] for a in jnp.split(ada, 6, axis=-1)]

    h = x * (1 + sc_a) + sh_a
    proj = jnp.einsum('btd,de->bte', h, w_in)
    sizes = [FOX_W, FOX_W, FOX_W, H_FOX, DIFF_W, DIFF_W, DIFF_W, CONV_CH, CONV_CH]
    splits = [int(s) for s in np.cumsum(sizes)[:-1]]
    fq, fk, fv, fg, dq, dk, dv, cv, cg = jnp.split(proj, splits, axis=-1)
    fq = fq.reshape(B, T, H_FOX, HEAD_DIM)
    fk = fk.reshape(B, T, H_FOX, HEAD_DIM)
    fv = fv.reshape(B, T, H_FOX, HEAD_DIM)
    logf = jax.nn.log_sigmoid((fg + b_fgate).astype(jnp.float32))
    dq = dq.reshape(B, T, H_DIFF, 2, D_DIFF)
    dk = dk.reshape(B, T, H_DIFF, 2, D_DIFF)
    dv = dv.reshape(B, T, H_DIFF, HEAD_DIM)

    fox_o, diff_pair = attend(fq, fk, fv, logf, dq, dk, dv)

    lam_init = 0.8 - 0.6 * math.exp(-0.3 * l)
    lam = (jnp.exp(jnp.sum(lq1.astype(jnp.float32) * lk1.astype(jnp.float32)))
           - jnp.exp(jnp.sum(lq2.astype(jnp.float32) * lk2.astype(jnp.float32))) + lam_init)
    diff_o = _rmsnorm(diff_pair[..., 0, :] - lam * diff_pair[..., 1, :], diff_g) * (1.0 - lam_init)

    conv_o, new_buf = _conformer_conv(cv, cg, conv_buf, conv_w, conv_b, conv_ln_g, conv_ln_b)

    mix = jnp.concatenate([fox_o.reshape(B, T, FOX_W).astype(x.dtype),
                           diff_o.reshape(B, T, DIFF_W).astype(x.dtype),
                           conv_o.astype(x.dtype)], axis=-1)
    y = jnp.einsum('bte,ed->btd', mix, w_o)
    x = _layernorm(DEEPNORM_ALPHA * x + (1 + g_a) * y, ln_mix_g, ln_mix_b)

    h = x * (1 + sc_f) + sh_f
    gate, up = jnp.split(jnp.einsum('btd,df->btf', h, w_ffn_in), 2, axis=-1)
    y = jnp.einsum('btf,fd->btd', jax.nn.silu(gate) * up, w_ffn_out)
    x = _layernorm(DEEPNORM_ALPHA * x + (1 + g_f) * y, ln_ffn_g, ln_ffn_b)
    return x, (fk, fv, logf, dk.reshape(B, T, H_DIFF, 2 * D_DIFF), dv, new_buf)


def setup_inputs(seed: int = 0) -> dict:
    key = jax.random.key(seed)
    ks = jax.random.split(key, 40)
    f32 = jnp.float32
    n_pages = PAST_LEN // PAGE_SIZE
    n_used = DEC_BATCH * n_pages
    n_pool = n_used + (n_used + 3) // 4

    def nrm(k, shape, s):
        return s * jax.random.normal(k, shape, f32)

    d = D_MODEL
    return {
        'x_prompt': nrm(ks[0], (BATCH, SEQ, d), 1.0),
        'x_sample': nrm(ks[1], (DEC_BATCH, DEC_SEQ, d), 1.0),
        'cache_fox_k': nrm(ks[2], (DEPTH, n_pool, PAGE_SIZE, H_FOX, HEAD_DIM), 1.0),
        'cache_fox_v': nrm(ks[3], (DEPTH, n_pool, PAGE_SIZE, H_FOX, HEAD_DIM), 1.0),
        'cache_fox_logf': jax.nn.log_sigmoid(3.0 + nrm(ks[4], (DEPTH, n_pool, PAGE_SIZE, H_FOX), 1.0)),
        'cache_diff_k': nrm(ks[5], (DEPTH, n_pool, PAGE_SIZE, H_DIFF, 2 * D_DIFF), 1.0),
        'cache_diff_v': nrm(ks[6], (DEPTH, n_pool, PAGE_SIZE, H_DIFF, HEAD_DIM), 1.0),
        'state_conv': nrm(ks[7], (DEPTH, DEC_BATCH, CONV_WIDTH - 1, CONV_CH), 0.5),
        'page_table': jax.random.permutation(ks[8], n_pool)[:n_used].reshape(DEC_BATCH, n_pages).astype(jnp.int32),
        'c_prompt': nrm(ks[9], (BATCH, d), 1.0),
        'c_sample': nrm(ks[10], (DEC_BATCH, d), 1.0),
        'w_ada': nrm(ks[11], (DEPTH, d, 6 * d), 0.1 * d ** -0.5),
        'b_ada': nrm(ks[12], (DEPTH, 6 * d), 0.01),
        'w_in': nrm(ks[13], (DEPTH, d, N_IN), d ** -0.5),
        'b_fgate': 3.0 + nrm(ks[14], (DEPTH, H_FOX), 0.5),
        'lambda_q1': nrm(ks[15], (DEPTH, D_DIFF), 0.1),
        'lambda_k1': nrm(ks[16], (DEPTH, D_DIFF), 0.1),
        'lambda_q2': nrm(ks[17], (DEPTH, D_DIFF), 0.1),
        'lambda_k2': nrm(ks[18], (DEPTH, D_DIFF), 0.1),
        'diff_norm_g': 1.0 + nrm(ks[19], (DEPTH, HEAD_DIM), 0.05),
        'conv_w': nrm(ks[20], (DEPTH, CONV_WIDTH, CONV_CH), CONV_WIDTH ** -0.5),
        'conv_b': nrm(ks[21], (DEPTH, CONV_CH), 0.01),
        'conv_ln_g': 1.0 + nrm(ks[22], (DEPTH, CONV_CH), 0.05),
        'conv_ln_b': nrm(ks[23], (DEPTH, CONV_CH), 0.01),
        'w_o': nrm(ks[24], (DEPTH, d, d), DEEPNORM_BETA * d ** -0.5),
        'ln_mix_g': 1.0 + nrm(ks[25], (DEPTH, d), 0.05),
        'ln_mix_b': nrm(ks[26], (DEPTH, d), 0.01),
        'w_ffn_in': nrm(ks[27], (DEPTH, d, 2 * D_FF), d ** -0.5),
        'w_ffn_out': nrm(ks[28], (DEPTH, D_FF, d), DEEPNORM_BETA * D_FF ** -0.5),
        'ln_ffn_g': 1.0 + nrm(ks[29], (DEPTH, d), 0.05),
        'ln_ffn_b': nrm(ks[30], (DEPTH, d), 0.01),
    }


def reference(x_prompt, x_sample, cache_fox_k, cache_fox_v, cache_fox_logf, cache_diff_k, cache_diff_v,
              state_conv, page_table, c_prompt, c_sample, w_ada, b_ada, w_in, b_fgate,
              lambda_q1, lambda_k1, lambda_q2, lambda_k2, diff_norm_g, conv_w, conv_b,
              conv_ln_g, conv_ln_b, w_o, ln_mix_g, ln_mix_b, w_ffn_in, w_ffn_out, ln_ffn_g, ln_ffn_b):
    dec_b, n_pages = page_table.shape
    past = n_pages * cache_fox_k.shape[2]
    xp, xs = x_prompt, x_sample
    new_p = [[] for _ in range(6)]
    new_s = [[] for _ in range(6)]
    for l in range(DEPTH):
        params = (w_ada[l], b_ada[l], w_in[l], b_fgate[l], lambda_q1[l], lambda_k1[l],
                  lambda_q2[l], lambda_k2[l], diff_norm_g[l], conv_w[l], conv_b[l],
                  conv_ln_g[l], conv_ln_b[l], w_o[l], ln_mix_g[l], ln_mix_b[l],
                  w_ffn_in[l], w_ffn_out[l], ln_ffn_g[l], ln_ffn_b[l])
        zero_buf = jnp.zeros((xp.shape[0], CONV_WIDTH - 1, CONV_CH), xp.dtype)
        xp, rows_p = _layer(l, xp, c_prompt, zero_buf, _prompt_attend, *params)
        kp = cache_fox_k[l][page_table].reshape(dec_b, past, H_FOX, HEAD_DIM)
        vp = cache_fox_v[l][page_table].reshape(dec_b, past, H_FOX, HEAD_DIM)
        lfp = cache_fox_logf[l][page_table].reshape(dec_b, past, H_FOX)
        dkp = cache_diff_k[l][page_table].reshape(dec_b, past, H_DIFF, 2 * D_DIFF)
        dvp = cache_diff_v[l][page_table].reshape(dec_b, past, H_DIFF, HEAD_DIM)
        attend_s = functools.partial(_sample_attend, kp, vp, lfp, dkp, dvp)
        xs, rows_s = _layer(l, xs, c_sample, state_conv[l], attend_s, *params)
        for i in range(6):
            new_p[i].append(rows_p[i])
            new_s[i].append(rows_s[i])
    return (xp, xs,
            jnp.stack(new_p[0]), jnp.stack(new_p[1]), jnp.stack(new_p[2]),
            jnp.stack(new_p[3]), jnp.stack(new_p[4]), jnp.stack(new_p[5]),
            jnp.stack(new_s[0]), jnp.stack(new_s[1]), jnp.stack(new_s[2]),
            jnp.stack(new_s[3]), jnp.stack(new_s[4]), jnp.stack(new_s[5]))
```

```python
import functools
import math

import jax
import jax.numpy as jnp
from jax import lax
from jax.experimental import pallas as pl
from jax.experimental.pallas import tpu as pltpu

F32 = jnp.float32
BF16 = jnp.bfloat16

HEAD_DIM = 64
H_FOX = 6
FOX_W = H_FOX * HEAD_DIM
H_DIFF = 4
D_DIFF = HEAD_DIM // 2
DIFF_W = H_DIFF * HEAD_DIM
QK_W = FOX_W + DIFF_W
CONV_WIDTH = 31
LN_EPS = 1e-5
NEG_INF = -1e30
LANES = 128
SUBLANES = 8
VMEM_LIMIT = 56 * 1024 * 1024

ROW_TILE = 512
ATTN_TILE = 256
FFN_CHUNK = 256
PAGES_PER_STEP = 8


def _nn(a, b):
    return jnp.dot(a, b, preferred_element_type=F32)


def _nt(a, b):
    return lax.dot_general(a, b, (((1,), (1,)), ((), ())), preferred_element_type=F32)


def _sigmoid(x):
    return jax.nn.sigmoid(x)


def _log_sigmoid(x):
    return jnp.minimum(x, 0.0) - jnp.log1p(jnp.exp(-jnp.abs(x)))


def _layernorm(x, g, b):
    mu = jnp.mean(x, axis=-1, keepdims=True)
    var = jnp.mean(jnp.square(x - mu), axis=-1, keepdims=True)
    return (x - mu) * lax.rsqrt(var + LN_EPS) * g + b


def _lambda(lam_ref, lam_init):
    a = lam_ref[...]
    s1 = jnp.sum(a[0:1] * a[1:2], axis=-1, keepdims=True)
    s2 = jnp.sum(a[2:3] * a[3:4], axis=-1, keepdims=True)
    return jnp.exp(s1) - jnp.exp(s2) + lam_init


def _const_spec(shape):
    nd = len(shape)
    return pl.BlockSpec(shape, lambda *_: (0,) * nd, pipeline_mode=pl.Buffered(1))


def _ada_kernel(c_ref, w_ref, b_ref, o_ref):
    c = c_ref[...]
    a = (c * _sigmoid(c)).astype(BF16)
    o_ref[...] = _nn(a, w_ref[...]) + b_ref[...]


def _ada(c_all, w_ada_bf, b_ada):
    depth, d, n = w_ada_bf.shape
    rows = c_all.shape[0]
    tn = d
    return pl.pallas_call(
        _ada_kernel,
        out_shape=jax.ShapeDtypeStruct((depth, rows, n), F32),
        grid=(depth, n // tn),
        in_specs=[pl.BlockSpec((rows, d), lambda l, j: (0, 0)),
                  pl.BlockSpec((None, d, tn), lambda l, j: (l, 0, j)),
                  pl.BlockSpec((None, 1, tn), lambda l, j: (l, 0, j))],
        out_specs=pl.BlockSpec((None, rows, tn), lambda l, j: (l, 0, j)),
        compiler_params=pltpu.CompilerParams(dimension_semantics=("arbitrary", "arbitrary"),
                                             vmem_limit_bytes=VMEM_LIMIT),
        name="ada",
    )(c_all, w_ada_bf, b_ada.reshape(depth, 1, n))


def _inproj_kernel(*refs, transposed, conv_ch, attn_tile):
    x_ref, sc_ref, sh_ref, w_ref, bfg_ref = refs[:5]
    h = (x_ref[...] * (1.0 + sc_ref[...]) + sh_ref[...]).astype(BF16)
    c0 = 3 * QK_W
    if transposed:
        (tri_ref, qT_ref, k_ref, vT_ref, fkT_ref, fvT_ref, dkT_ref, dvT_ref, u_ref, lfT_ref,
         cum_ref, cumT_ref, carry_ref) = refs[5:]
        tm = x_ref.shape[0]
        qT_ref[...] = _nn(h, w_ref[:, 0:QK_W]).T.astype(BF16)
        pk = _nn(h, w_ref[:, QK_W:2 * QK_W])
        k_ref[...] = pk.astype(BF16)
        pkT = pk.T
        fkT_ref[...] = pkT[0:FOX_W]
        dkT_ref[...] = pkT[FOX_W:QK_W]
        pvT = _nn(h, w_ref[:, 2 * QK_W:3 * QK_W]).T
        fvT_ref[...] = pvT[0:FOX_W]
        dvT_ref[...] = pvT[FOX_W:QK_W]
        pvT_bf = pvT.astype(BF16)
        for t in range(tm // attn_tile):
            vT_ref[t] = pvT_bf[:, t * attn_tile:(t + 1) * attn_tile]
    else:
        q_ref, k_ref, v_ref, u_ref, lf_ref = refs[5:]
        q_ref[...] = _nn(h, w_ref[:, 0:QK_W])
        k_ref[...] = _nn(h, w_ref[:, QK_W:2 * QK_W])
        v_ref[...] = _nn(h, w_ref[:, 2 * QK_W:3 * QK_W])
    pc = _nn(h, w_ref[:, c0:c0 + 2 * conv_ch])
    u_ref[...] = pc[:, 0:conv_ch] * _sigmoid(pc[:, conv_ch:2 * conv_ch])
    z = _nn(h, w_ref[:, c0 + 2 * conv_ch:c0 + 2 * conv_ch + LANES]) + bfg_ref[...]
    lane = lax.broadcasted_iota(jnp.int32, z.shape, 1)
    lf = jnp.where(lane < H_FOX, _log_sigmoid(z), 0.0)
    if not transposed:
        lf_ref[...] = lf
        return
    @pl.when(pl.program_id(1) == 0)
    def _():
        carry_ref[...] = jnp.zeros_like(carry_ref)
    hi = lf.astype(BF16)
    r1 = lf - hi.astype(F32)
    mid = r1.astype(BF16)
    lo = (r1 - mid.astype(F32)).astype(BF16)
    tri = tri_ref[...]
    cum = _nn(tri, hi) + _nn(tri, mid) + _nn(tri, lo) + carry_ref[...]
    carry_ref[...] = cum[tm - 1:tm, :]
    cum_ref[...] = cum
    cumT_ref[...] = cum.T[0:SUBLANES, :]
    lfT_ref[...] = lf.T[0:SUBLANES, :]


def _inproj_prompt(x, sc, sh, w1, bfg, tri):
    b, s, d = x.shape
    conv_ch = d - QK_W
    tm = min(ROW_TILE, s)
    ta = min(ATTN_TILE, s)
    nb = s // ta
    row = lambda w: pl.BlockSpec((None, tm, w), lambda i, j: (i, j, 0))
    colT = lambda r: pl.BlockSpec((None, r, tm), lambda i, j: (i, 0, j))
    vec = pl.BlockSpec((None, 1, d), lambda i, j: (i, 0, 0))
    out_shape = (
        jax.ShapeDtypeStruct((b, QK_W, s), BF16),
        jax.ShapeDtypeStruct((b, s, QK_W), BF16),
        jax.ShapeDtypeStruct((b, nb, QK_W, ta), BF16),
        jax.ShapeDtypeStruct((b, FOX_W, s), F32),
        jax.ShapeDtypeStruct((b, FOX_W, s), F32),
        jax.ShapeDtypeStruct((b, DIFF_W, s), F32),
        jax.ShapeDtypeStruct((b, DIFF_W, s), F32),
        jax.ShapeDtypeStruct((b, s, conv_ch), F32),
        jax.ShapeDtypeStruct((b, SUBLANES, s), F32),
        jax.ShapeDtypeStruct((b, s, LANES), F32),
        jax.ShapeDtypeStruct((b, SUBLANES, s), F32),
    )
    out_specs = (colT(QK_W), row(QK_W),
                 pl.BlockSpec((None, tm // ta, QK_W, ta), lambda i, j: (i, j, 0, 0)),
                 colT(FOX_W), colT(FOX_W), colT(DIFF_W), colT(DIFF_W), row(conv_ch),
                 colT(SUBLANES), row(LANES), colT(SUBLANES))
    return pl.pallas_call(
        functools.partial(_inproj_kernel, transposed=True, conv_ch=conv_ch, attn_tile=ta),
        out_shape=out_shape,
        grid=(b, s // tm),
        in_specs=[row(d), vec, vec, _const_spec(w1.shape), _const_spec(bfg.shape), _const_spec(tri.shape)],
        out_specs=out_specs,
        scratch_shapes=[pltpu.VMEM((1, LANES), F32)],
        compiler_params=pltpu.CompilerParams(dimension_semantics=("arbitrary", "arbitrary"),
                                             vmem_limit_bytes=VMEM_LIMIT),
        name="inproj_prompt",
    )(x, sc, sh, w1, bfg, tri)


def _inproj_sample(x, sc, sh, w1, bfg):
    n, d = x.shape
    conv_ch = d - QK_W
    out_shape = (jax.ShapeDtypeStruct((n, QK_W), F32), jax.ShapeDtypeStruct((n, QK_W), F32),
                 jax.ShapeDtypeStruct((n, QK_W), F32), jax.ShapeDtypeStruct((n, conv_ch), F32),
                 jax.ShapeDtypeStruct((n, LANES), F32))
    return pl.pallas_call(
        functools.partial(_inproj_kernel, transposed=False, conv_ch=conv_ch, attn_tile=0),
        out_shape=out_shape,
        compiler_params=pltpu.CompilerParams(vmem_limit_bytes=VMEM_LIMIT),
        name="inproj_sample",
    )(x, sc, sh, w1, bfg)


def _prompt_attn_kernel(qT_ref, k_ref, vT_ref, cum_ref, cumT_ref, lam_ref, g_ref, o_ref,
                        m_scr, l_scr, acc_scr, *, lam_init):
    i = pl.program_id(1)
    t = qT_ref.shape[1]
    key_idx = lax.broadcasted_iota(jnp.int32, (t, t), 0)
    qry_idx = lax.broadcasted_iota(jnp.int32, (t, t), 1)
    causal = key_idx <= qry_idx
    row128 = lax.broadcasted_iota(jnp.int32, (LANES, t), 0)

    def attend(qTm, koff, voff, scale, head):
        m_scr[...] = jnp.full_like(m_scr, NEG_INF)
        l_scr[...] = jnp.zeros_like(l_scr)
        acc_scr[...] = jnp.zeros_like(acc_scr)
        cq = None if head is None else cumT_ref[head:head + 1, :]

        def step(j, masked):
            start = pl.multiple_of(j * t, t)
            k = k_ref[pl.ds(start, t), koff:koff + LANES]
            s = _nn(k, qTm)
            if scale != 1.0:
                s = s * scale
            if head is not None:
                s = (s + cq) - cum_ref[pl.ds(start, t), head:head + 1]
            if masked:
                s = jnp.where(causal, s, NEG_INF)
            m_prev = m_scr[...]
            m_new = jnp.maximum(m_prev, jnp.max(s, axis=0, keepdims=True))
            alpha = jnp.exp(m_prev - m_new)
            p = jnp.exp(s - m_new)
            l_scr[...] = alpha * l_scr[...] + jnp.sum(p, axis=0, keepdims=True)
            vT = vT_ref[j, voff:voff + LANES, :]
            acc_scr[...] = alpha * acc_scr[...] + _nn(vT, p.astype(BF16))
            m_scr[...] = m_new

        def body(j, carry):
            step(j, False)
            return carry

        lax.fori_loop(0, i, body, 0)
        step(i, True)
        return acc_scr[...] / l_scr[...]

    for pr in range(H_FOX // 2):
        off = pr * LANES
        qT = qT_ref[off:off + LANES, :].astype(F32)
        outs = []
        for hh in range(2):
            qTm = jnp.where((row128 // HEAD_DIM) == hh, qT, 0.0).astype(BF16)
            outs.append(attend(qTm, off, off, 1.0, 2 * pr + hh))
        oT = jnp.where(row128 < HEAD_DIM, outs[0], outs[1])
        o_ref[:, off:off + LANES] = oT.T.astype(o_ref.dtype)

    lam = _lambda(lam_ref, lam_init)
    for pr in range(H_DIFF // 2):
        off = FOX_W + pr * LANES
        qT = qT_ref[off:off + LANES, :].astype(F32)
        outs = []
        for mp in range(4):
            qTm = jnp.where((row128 // D_DIFF) == mp, qT, 0.0).astype(BF16)
            outs.append(attend(qTm, off, off, D_DIFF ** -0.5, None))
        dT = jnp.where(row128 < HEAD_DIM, outs[0] - lam * outs[1], outs[2] - lam * outs[3])
        sq = dT * dT
        ms0 = jnp.mean(sq[0:HEAD_DIM], axis=0, keepdims=True)
        ms1 = jnp.mean(sq[HEAD_DIM:LANES], axis=0, keepdims=True)
        ms = jnp.where(row128 < HEAD_DIM, ms0, ms1)
        dn = (dT * lax.rsqrt(ms + LN_EPS)).T * g_ref[...] * (1.0 - lam_init)
        o_ref[:, off:off + LANES] = dn.astype(o_ref.dtype)


def _prompt_attn(qT, k, vT, cum, cumT, lam4, g128, lam_init):
    b, s, _ = k.shape
    t = vT.shape[-1]
    return pl.pallas_call(
        functools.partial(_prompt_attn_kernel, lam_init=lam_init),
        out_shape=jax.ShapeDtypeStruct((b, s, QK_W), BF16),
        grid=(b, s // t),
        in_specs=[pl.BlockSpec((None, QK_W, t), lambda bi, i: (bi, 0, i)),
                  pl.BlockSpec((None, s, QK_W), lambda bi, i: (bi, 0, 0)),
                  pl.BlockSpec((None, s // t, QK_W, t), lambda bi, i: (bi, 0, 0, 0)),
                  pl.BlockSpec((None, s, LANES), lambda bi, i: (bi, 0, 0)),
                  pl.BlockSpec((None, SUBLANES, t), lambda bi, i: (bi, 0, i)),
                  _const_spec(lam4.shape), _const_spec(g128.shape)],
        out_specs=pl.BlockSpec((None, t, QK_W), lambda bi, i: (bi, i, 0)),
        scratch_shapes=[pltpu.VMEM((1, t), F32), pltpu.VMEM((1, t), F32), pltpu.VMEM((LANES, t), F32)],
        compiler_params=pltpu.CompilerParams(dimension_semantics=("arbitrary", "arbitrary"),
                                             vmem_limit_bytes=VMEM_LIMIT),
        name="prompt_attn",
    )(qT, k, vT, cum, cumT, lam4, g128)


HALO = 32


def _conv_finish(y, cb_ref, g_ref, b_ref):
    y = _layernorm(y + cb_ref[...], g_ref[...], b_ref[...])
    return y * _sigmoid(y)


def _conv_prompt_kernel(u_ref, halo_ref, cw_ref, cb_ref, g_ref, b_ref, o_ref, win_scr):
    tm = u_ref.shape[0]
    j = pl.program_id(1)
    halo = halo_ref[...]
    win_scr[0:HALO, :] = jnp.where(j == 0, jnp.zeros_like(halo), halo)
    win_scr[HALO:HALO + tm, :] = u_ref[...]
    base = HALO - (CONV_WIDTH - 1)
    acc = win_scr[base:base + tm, :] * cw_ref[0:1, :]
    for w in range(1, CONV_WIDTH):
        acc = acc + win_scr[base + w:base + w + tm, :] * cw_ref[w:w + 1, :]
    o_ref[...] = _conv_finish(acc, cb_ref, g_ref, b_ref).astype(o_ref.dtype)


def _conv_prompt(u, cw, cb, g, bb):
    b, s, c = u.shape
    tm = min(ROW_TILE, s)
    hb = tm // HALO
    return pl.pallas_call(
        _conv_prompt_kernel,
        out_shape=jax.ShapeDtypeStruct((b, s, c), BF16),
        grid=(b, s // tm),
        in_specs=[pl.BlockSpec((None, tm, c), lambda i, j: (i, j, 0)),
                  pl.BlockSpec((None, HALO, c), lambda i, j: (i, jnp.maximum(j * hb - 1, 0), 0)),
                  _const_spec(cw.shape), _const_spec(cb.shape), _const_spec(g.shape), _const_spec(bb.shape)],
        out_specs=pl.BlockSpec((None, tm, c), lambda i, j: (i, j, 0)),
        scratch_shapes=[pltpu.VMEM((HALO + tm, c), F32)],
        compiler_params=pltpu.CompilerParams(dimension_semantics=("arbitrary", "arbitrary"),
                                             vmem_limit_bytes=VMEM_LIMIT),
        name="conv_prompt",
    )(u, u, cw, cb, g, bb)


def _conv_sample_kernel(buf_ref, u_ref, cw_ref, cb_ref, g_ref, b_ref, o_ref):
    acc = u_ref[...] * cw_ref[CONV_WIDTH - 1:CONV_WIDTH, :]
    for w in range(CONV_WIDTH - 1):
        acc = acc + buf_ref[w] * cw_ref[w:w + 1, :]
    o_ref[...] = _conv_finish(acc, cb_ref, g_ref, b_ref).astype(o_ref.dtype)


def _conv_sample(buf_t, u, cw, cb, g, bb):
    return pl.pallas_call(
        _conv_sample_kernel,
        out_shape=jax.ShapeDtypeStruct(u.shape, BF16),
        name="conv_sample",
    )(buf_t, u, cw, cb, g, bb)


def _post_kernel(x_ref, ao_ref, co_ref, ga_ref, shf_ref, scf_ref, gf_ref, woa_ref, woc_ref,
                 g1_ref, b1_ref, win_ref, wout_ref, g2_ref, b2_ref, o_ref, a_scr, *, alpha, d_ff):
    y = _nn(ao_ref[...], woa_ref[...]) + _nn(co_ref[...], woc_ref[...])
    x1 = _layernorm(alpha * x_ref[...] + (1.0 + ga_ref[...]) * y, g1_ref[...], b1_ref[...])
    h = (x1 * (1.0 + scf_ref[...]) + shf_ref[...]).astype(BF16)
    for c in range(d_ff // FFN_CHUNK):
        lo = c * FFN_CHUNK
        gate = _nn(h, win_ref[:, lo:lo + FFN_CHUNK])
        up = _nn(h, win_ref[:, d_ff + lo:d_ff + lo + FFN_CHUNK])
        a_scr[:, lo:lo + FFN_CHUNK] = (gate * _sigmoid(gate) * up).astype(BF16)
    y2 = _nn(a_scr[...], wout_ref[...])
    o_ref[...] = _layernorm(alpha * x1 + (1.0 + gf_ref[...]) * y2, g2_ref[...], b2_ref[...])


def _post(x, ao, co, mods, woa, woc, g1, b1, w_in, w_out, g2, b2, alpha):
    gdim, r, d = x.shape
    d_ff = w_out.shape[0]
    tm = min(ROW_TILE, r)
    row = lambda w: pl.BlockSpec((None, tm, w), lambda i, j: (i, j, 0))

    def mod_spec(m):
        if m.shape[1] == 1:
            return pl.BlockSpec((None, 1, d), lambda i, j: (i, 0, 0))
        return row(d)

    consts = (woa, woc, g1, b1, w_in, w_out, g2, b2)
    return pl.pallas_call(
        functools.partial(_post_kernel, alpha=alpha, d_ff=d_ff),
        out_shape=jax.ShapeDtypeStruct(x.shape, F32),
        grid=(gdim, r // tm),
        in_specs=[row(d), row(ao.shape[-1]), row(co.shape[-1])] + [mod_spec(m) for m in mods]
                 + [_const_spec(c.shape) for c in consts],
        out_specs=row(d),
        scratch_shapes=[pltpu.VMEM((tm, d_ff), BF16)],
        compiler_params=pltpu.CompilerParams(dimension_semantics=("arbitrary", "arbitrary"),
                                             vmem_limit_bytes=VMEM_LIMIT),
        name="post",
    )(x, ao, co, *mods, *consts)


def _decode_kernel(pt_ref, q_ref, kn_ref, vn_ref, lfn_ref, lam_ref, g_ref, utri_ref,
                   fk_hbm, fv_hbm, lf_hbm, dk_hbm, dv_hbm, o_ref,
                   fk_buf, fv_buf, lf_buf, dk_buf, dv_buf, sem,
                   mf_scr, lsf_scr, accf_scr, md_scr, lsd_scr, accd_scr, carry_scr,
                   *, layer, n_pages, lam_init):
    npg = PAGES_PER_STEP
    b = pl.program_id(0)
    c = pl.program_id(1)
    n_chunks = pl.num_programs(1)
    step = b * n_chunks + c
    n_steps = pl.num_programs(0) * n_chunks
    slot = step % 2
    hbm = (fk_hbm, fv_hbm, lf_hbm, dk_hbm, dv_hbm)
    bufs = (fk_buf, fv_buf, lf_buf, dk_buf, dv_buf)

    def copies(st, sl):
        bb = st // n_chunks
        cc = st % n_chunks
        first = bb * n_pages + (n_chunks - 1 - cc) * npg
        out = []
        for p in range(npg):
            page = pt_ref[first + p]
            for a in range(5):
                out.append(pltpu.make_async_copy(hbm[a].at[layer, page], bufs[a].at[sl, p], sem.at[sl, a]))
        return out

    @pl.when(step == 0)
    def _():
        for cp in copies(step, slot):
            cp.start()

    @pl.when(step + 1 < n_steps)
    def _():
        for cp in copies(step + 1, 1 - slot):
            cp.start()

    row_f = lax.broadcasted_iota(jnp.int32, (SUBLANES, FOX_W), 0)
    lane_f = lax.broadcasted_iota(jnp.int32, (SUBLANES, FOX_W), 1)
    own_f = (lane_f // HEAD_DIM) == row_f
    row_d = lax.broadcasted_iota(jnp.int32, (SUBLANES, DIFF_W), 0)
    lane_d = lax.broadcasted_iota(jnp.int32, (SUBLANES, DIFF_W), 1)
    own_dq = (lane_d // D_DIFF) == row_d
    own_dv = (lane_d // HEAD_DIM) == (row_d // 2)

    q = q_ref[...]
    qf = jnp.where(own_f, q[:, 0:FOX_W], 0.0).astype(BF16)
    qd = jnp.where(own_dq, q[:, FOX_W:QK_W], 0.0).astype(BF16)
    d_scale = D_DIFF ** -0.5

    @pl.when(c == 0)
    def _():
        kn = kn_ref[...].astype(BF16).astype(F32)
        vn = vn_ref[...].astype(BF16).astype(F32)
        mf_scr[...] = jnp.sum(qf.astype(F32) * kn[:, 0:FOX_W], axis=-1, keepdims=True)
        md_scr[...] = jnp.sum(qd.astype(F32) * kn[:, FOX_W:QK_W], axis=-1, keepdims=True) * d_scale
        lsf_scr[...] = jnp.ones_like(lsf_scr)
        lsd_scr[...] = jnp.ones_like(lsd_scr)
        accf_scr[...] = jnp.broadcast_to(vn[:, 0:FOX_W], accf_scr.shape)
        accd_scr[...] = jnp.broadcast_to(vn[:, FOX_W:QK_W], accd_scr.shape)
        carry_scr[...] = lfn_ref[...]

    for cp in copies(step, slot):
        cp.wait()

    utri = utri_ref[...]
    carry = carry_scr[...]
    bias = [None] * npg
    for p in reversed(range(npg)):
        lf = lf_buf[slot, p]
        hi = lf.astype(BF16)
        r1 = lf - hi.astype(F32)
        mid = r1.astype(BF16)
        lo = (r1 - mid.astype(F32)).astype(BF16)
        bias[p] = _nn(hi, utri) + _nn(mid, utri) + _nn(lo, utri) + carry
        carry = carry + jnp.sum(lf, axis=-1, keepdims=True)
    carry_scr[...] = carry

    sf = [_nn(qf, fk_buf[slot, p].astype(BF16)) + bias[p] for p in range(npg)]
    sd = [_nn(qd, dk_buf[slot, p].astype(BF16)) * d_scale for p in range(npg)]

    def online(s_list, m_scr, l_scr, acc_scr, v_buf):
        m_prev = m_scr[...]
        m_new = m_prev
        for s in s_list:
            m_new = jnp.maximum(m_new, jnp.max(s, axis=-1, keepdims=True))
        alpha = jnp.exp(m_prev - m_new)
        l_new = alpha * l_scr[...]
        acc = alpha * acc_scr[...]
        for p, s in enumerate(s_list):
            e = jnp.exp(s - m_new)
            l_new = l_new + jnp.sum(e, axis=-1, keepdims=True)
            acc = acc + _nt(e.astype(BF16), v_buf[slot, p].astype(BF16))
        m_scr[...] = m_new
        l_scr[...] = l_new
        acc_scr[...] = acc

    online(sf, mf_scr, lsf_scr, accf_scr, fv_buf)
    online(sd, md_scr, lsd_scr, accd_scr, dv_buf)

    @pl.when(c == n_chunks - 1)
    def _():
        zf = accf_scr[...] / lsf_scr[...]
        fox = jnp.sum(jnp.where(own_f, zf, 0.0), axis=0, keepdims=True)
        lam = _lambda(lam_ref, lam_init)
        coef = jnp.where((row_d % 2) == 0, 1.0, -lam)
        zd = coef * (accd_scr[...] / lsd_scr[...])
        dif = jnp.sum(jnp.where(own_dv, zd, 0.0), axis=0, keepdims=True)
        sq = dif * dif
        lane1 = lax.broadcasted_iota(jnp.int32, (1, DIFF_W), 1)
        ms = jnp.zeros_like(dif)
        for hd in range(H_DIFF):
            in_head = (lane1 // HEAD_DIM) == hd
            ms_h = jnp.sum(jnp.where(in_head, sq, 0.0), axis=-1, keepdims=True) / HEAD_DIM
            ms = jnp.where(in_head, ms_h, ms)
        dn = dif * lax.rsqrt(ms + LN_EPS) * g_ref[...] * (1.0 - lam_init)
        o_ref[:, 0:FOX_W] = fox.astype(o_ref.dtype)
        o_ref[:, FOX_W:QK_W] = dn.astype(o_ref.dtype)


def _decode_attn(page_table, q, kn, vn, lfn, lam4, g256, utri, fkT, fvT, lfT, dkT, dvT, layer, lam_init):
    nb, n_pages = page_table.shape
    npg = PAGES_PER_STEP
    n_chunks = n_pages // npg
    page = fkT.shape[-1]
    per_b = lambda w: pl.BlockSpec((None, 1, w), lambda b, c, pt: (b, 0, 0))
    any_spec = pl.BlockSpec(memory_space=pl.ANY)
    grid_spec = pltpu.PrefetchScalarGridSpec(
        num_scalar_prefetch=1,
        grid=(nb, n_chunks),
        in_specs=[per_b(QK_W), per_b(QK_W), per_b(QK_W),
                  pl.BlockSpec((None, SUBLANES, LANES), lambda b, c, pt: (b, 0, 0)),
                  pl.BlockSpec(lam4.shape, lambda b, c, pt: (0, 0)),
                  pl.BlockSpec(g256.shape, lambda b, c, pt: (0, 0)),
                  pl.BlockSpec(utri.shape, lambda b, c, pt: (0, 0)),
                  any_spec, any_spec, any_spec, any_spec, any_spec],
        out_specs=per_b(QK_W),
        scratch_shapes=[pltpu.VMEM((2, npg, FOX_W, page), F32), pltpu.VMEM((2, npg, FOX_W, page), F32),
                        pltpu.VMEM((2, npg, SUBLANES, page), F32),
                        pltpu.VMEM((2, npg, DIFF_W, page), F32), pltpu.VMEM((2, npg, DIFF_W, page), F32),
                        pltpu.SemaphoreType.DMA((2, 5)),
                        pltpu.VMEM((SUBLANES, 1), F32), pltpu.VMEM((SUBLANES, 1), F32),
                        pltpu.VMEM((SUBLANES, FOX_W), F32),
                        pltpu.VMEM((SUBLANES, 1), F32), pltpu.VMEM((SUBLANES, 1), F32),
                        pltpu.VMEM((SUBLANES, DIFF_W), F32),
                        pltpu.VMEM((SUBLANES, LANES), F32)],
    )
    return pl.pallas_call(
        functools.partial(_decode_kernel, layer=layer, n_pages=n_pages, lam_init=lam_init),
        out_shape=jax.ShapeDtypeStruct((nb, 1, QK_W), BF16),
        grid_spec=grid_spec,
        compiler_params=pltpu.CompilerParams(dimension_semantics=("arbitrary", "arbitrary"),
                                             vmem_limit_bytes=VMEM_LIMIT),
        name="decode_attn",
    )(page_table.reshape(-1), q, kn, vn, lfn, lam4, g256, utri, fkT, fvT, lfT, dkT, dvT)


def _pack_w_in(w):
    o = 0
    parts = {}
    for name, width in (("fq", FOX_W), ("fk", FOX_W), ("fv", FOX_W), ("fg", H_FOX), ("dq", DIFF_W),
                        ("dk", DIFF_W), ("dv", DIFF_W)):
        parts[name] = w[:, o:o + width]
        o += width
    conv = w[:, o:]
    fg = jnp.pad(parts["fg"], ((0, 0), (0, LANES - H_FOX)))
    packed = jnp.concatenate([parts["fq"] * (HEAD_DIM ** -0.5), parts["dq"], parts["fk"], parts["dk"],
                              parts["fv"], parts["dv"], conv, fg], axis=1)
    return packed.astype(BF16)


def kernel(x_prompt, x_sample, cache_fox_k, cache_fox_v, cache_fox_logf, cache_diff_k, cache_diff_v, state_conv, page_table, c_prompt, c_sample, w_ada, b_ada, w_in, b_fgate, lambda_q1, lambda_k1, lambda_q2, lambda_k2, diff_norm_g, conv_w, conv_b, conv_ln_g, conv_ln_b, w_o, ln_mix_g, ln_mix_b, w_ffn_in, w_ffn_out, ln_ffn_g, ln_ffn_b):
    depth = w_in.shape[0]
    b, s, d = x_prompt.shape
    nb = x_sample.shape[0]
    conv_ch = d - QK_W
    n_pool, page = cache_fox_k.shape[1:3]
    alpha = (2.0 * depth) ** 0.25

    fkT = jnp.transpose(cache_fox_k, (0, 1, 3, 4, 2)).reshape(depth, n_pool, FOX_W, page)
    fvT = jnp.transpose(cache_fox_v, (0, 1, 3, 4, 2)).reshape(depth, n_pool, FOX_W, page)
    dkT = jnp.transpose(cache_diff_k, (0, 1, 3, 4, 2)).reshape(depth, n_pool, DIFF_W, page)
    dvT = jnp.transpose(cache_diff_v, (0, 1, 3, 4, 2)).reshape(depth, n_pool, DIFF_W, page)
    lfT = jnp.pad(jnp.transpose(cache_fox_logf, (0, 1, 3, 2)), ((0, 0), (0, 0), (0, SUBLANES - H_FOX), (0, 0)))
    conv_hist = jnp.transpose(state_conv, (0, 2, 1, 3))

    ada = _ada(jnp.concatenate([c_prompt, c_sample], axis=0), w_ada.astype(BF16), b_ada)

    tm = min(ROW_TILE, s)
    tri = jnp.tril(jnp.ones((tm, tm), F32)).astype(BF16)
    utri = jnp.triu(jnp.ones((page, page), F32), 1).T.astype(BF16)
    xp, xs = x_prompt, x_sample.reshape(nb, d)
    new_p = [[] for _ in range(6)]
    new_s = [[] for _ in range(6)]
    for l in range(depth):
        lam_init = 0.8 - 0.6 * math.exp(-0.3 * l)
        w1 = _pack_w_in(w_in[l])
        bfg = jnp.pad(b_fgate[l], (0, LANES - H_FOX)).reshape(1, LANES)
        lam4 = jnp.stack([lambda_q1[l], lambda_k1[l], lambda_q2[l], lambda_k2[l]])
        g128 = jnp.tile(diff_norm_g[l], 2).reshape(1, LANES)
        g256 = jnp.tile(diff_norm_g[l], H_DIFF).reshape(1, DIFF_W)
        cw, cb = conv_w[l], conv_b[l].reshape(1, conv_ch)
        cg, cbb = conv_ln_g[l].reshape(1, conv_ch), conv_ln_b[l].reshape(1, conv_ch)
        wo = w_o[l].astype(BF16)
        woa, woc = wo[0:QK_W], wo[QK_W:]
        wfi, wfo = w_ffn_in[l].astype(BF16), w_ffn_out[l].astype(BF16)
        g1, b1 = ln_mix_g[l].reshape(1, d), ln_mix_b[l].reshape(1, d)
        g2, b2 = ln_ffn_g[l].reshape(1, d), ln_ffn_b[l].reshape(1, d)
        mods = [ada[l, :, i * d:(i + 1) * d] for i in range(6)]
        mp = [m[0:b].reshape(b, 1, d) for m in mods]
        ms = [m[b:b + nb].reshape(1, nb, d) for m in mods]

        (qT, kr, vT, fk_t, fv_t, dk_t, dv_t, u, lf_t, cum, cumT) = _inproj_prompt(xp, mp[1], mp[0], w1, bfg, tri)
        ao = _prompt_attn(qT, kr, vT, cum, cumT, lam4, g128, lam_init)
        co = _conv_prompt(u, cw, cb, cg, cbb)
        xp = _post(xp, ao, co, (mp[2], mp[3], mp[4], mp[5]), woa, woc, g1, b1, wfi, wfo, g2, b2, alpha)
        unT = lambda a, h: jnp.transpose(a.reshape(b, h, HEAD_DIM, s), (0, 3, 1, 2))
        new_p[0].append(unT(fk_t, H_FOX))
        new_p[1].append(unT(fv_t, H_FOX))
        new_p[2].append(jnp.transpose(lf_t[:, 0:H_FOX, :], (0, 2, 1)))
        new_p[3].append(unT(dk_t, H_DIFF))
        new_p[4].append(unT(dv_t, H_DIFF))
        new_p[5].append(u[:, s - (CONV_WIDTH - 1):, :])

        qs, ks, vs, us, lfs = _inproj_sample(xs, ms[1][0], ms[0][0], w1, bfg)
        lfn = jnp.broadcast_to(jnp.pad(lfs[:, 0:H_FOX], ((0, 0), (0, SUBLANES - H_FOX)))[:, :, None],
                               (nb, SUBLANES, LANES))
        r3 = lambda a: a.reshape(nb, 1, a.shape[-1])
        aos = _decode_attn(page_table, r3(qs), r3(ks), r3(vs), lfn, lam4, g256, utri,
                           fkT, fvT, lfT, dkT, dvT, l, lam_init)
        cos = _conv_sample(conv_hist[l], us, cw, cb, cg, cbb)
        xs = _post(xs.reshape(1, nb, d), aos.reshape(1, nb, QK_W), cos.reshape(1, nb, conv_ch),
                   (ms[2], ms[3], ms[4], ms[5]), woa, woc, g1, b1, wfi, wfo, g2, b2, alpha).reshape(nb, d)
        new_s[0].append(ks[:, 0:FOX_W].reshape(nb, 1, H_FOX, HEAD_DIM))
        new_s[1].append(vs[:, 0:FOX_W].reshape(nb, 1, H_FOX, HEAD_DIM))
        new_s[2].append(lfs[:, 0:H_FOX].reshape(nb, 1, H_FOX))
        new_s[3].append(ks[:, FOX_W:QK_W].reshape(nb, 1, H_DIFF, HEAD_DIM))
        new_s[4].append(vs[:, FOX_W:QK_W].reshape(nb, 1, H_DIFF, HEAD_DIM))
        new_s[5].append(jnp.concatenate([state_conv[l][:, 1:, :], us[:, None, :]], axis=1))

    return (xp, xs.reshape(nb, 1, d),
            jnp.stack(new_p[0]), jnp.stack(new_p[1]), jnp.stack(new_p[2]),
            jnp.stack(new_p[3]), jnp.stack(new_p[4]), jnp.stack(new_p[5]),
            jnp.stack(new_s[0]), jnp.stack(new_s[1]), jnp.stack(new_s[2]),
            jnp.stack(new_s[3]), jnp.stack(new_s[4]), jnp.stack(new_s[5]))
```

```python
import functools
import math

import jax
import jax.numpy as jnp
from jax import lax
from jax.experimental import pallas as pl
from jax.experimental.pallas import tpu as pltpu

F32 = jnp.float32
BF16 = jnp.bfloat16

HEAD_DIM = 64
H_FOX = 6
FOX_W = H_FOX * HEAD_DIM
H_DIFF = 4
D_DIFF = HEAD_DIM // 2
DIFF_W = H_DIFF * HEAD_DIM
QK_W = FOX_W + DIFF_W
CONV_WIDTH = 31
LN_EPS = 1e-5
NEG_INF = -1e30
LANES = 128
SUBLANES = 8
VMEM_LIMIT = 56 * 1024 * 1024

ROW_TILE = 512
ATTN_TILE = 256
FFN_CHUNK = 256
PAGES_PER_STEP = 16


def _nn(a, b):
    return jnp.dot(a, b, preferred_element_type=F32)


def _nt(a, b):
    return lax.dot_general(a, b, (((1,), (1,)), ((), ())), preferred_element_type=F32)


def _sigmoid(x):
    return jax.nn.sigmoid(x)


def _log_sigmoid(x):
    return jnp.minimum(x, 0.0) - jnp.log1p(jnp.exp(-jnp.abs(x)))


def _layernorm(x, g, b):
    mu = jnp.mean(x, axis=-1, keepdims=True)
    var = jnp.mean(jnp.square(x - mu), axis=-1, keepdims=True)
    return (x - mu) * lax.rsqrt(var + LN_EPS) * g + b


def _lambda(lam_ref, lam_init):
    a = lam_ref[...]
    s1 = jnp.sum(a[0:1] * a[1:2], axis=-1, keepdims=True)
    s2 = jnp.sum(a[2:3] * a[3:4], axis=-1, keepdims=True)
    return jnp.exp(s1) - jnp.exp(s2) + lam_init


def _const_spec(shape):
    nd = len(shape)
    return pl.BlockSpec(shape, lambda *_: (0,) * nd, pipeline_mode=pl.Buffered(1))


def _ada_kernel(c_ref, w_ref, b_ref, o_ref):
    c = c_ref[...]
    a = (c * _sigmoid(c)).astype(BF16)
    o_ref[...] = _nn(a, w_ref[...]) + b_ref[...]


def _ada(c_all, w_ada_bf, b_ada):
    depth, d, n = w_ada_bf.shape
    rows = c_all.shape[0]
    tn = d
    return pl.pallas_call(
        _ada_kernel,
        out_shape=jax.ShapeDtypeStruct((depth, rows, n), F32),
        grid=(depth, n // tn),
        in_specs=[pl.BlockSpec((rows, d), lambda l, j: (0, 0)),
                  pl.BlockSpec((None, d, tn), lambda l, j: (l, 0, j)),
                  pl.BlockSpec((None, 1, tn), lambda l, j: (l, 0, j))],
        out_specs=pl.BlockSpec((None, rows, tn), lambda l, j: (l, 0, j)),
        compiler_params=pltpu.CompilerParams(dimension_semantics=("arbitrary", "arbitrary"),
                                             vmem_limit_bytes=VMEM_LIMIT),
        name="ada",
    )(c_all, w_ada_bf, b_ada.reshape(depth, 1, n))


def _inproj_kernel(*refs, transposed, conv_ch, attn_tile):
    x_ref, sc_ref, sh_ref, w_ref, bfg_ref = refs[:5]
    h = (x_ref[...] * (1.0 + sc_ref[...]) + sh_ref[...]).astype(BF16)
    c0 = 3 * QK_W
    proj = _nn(h, w_ref[...])
    if transposed:
        (tri_ref, qT_ref, k_ref, vT_ref, fkT_ref, fvT_ref, dkT_ref, dvT_ref, u_ref, lfT_ref,
         cum_ref, cumT_ref, carry_ref) = refs[5:]
        tm = x_ref.shape[0]
        qT_ref[...] = proj[:, 0:QK_W].T.astype(BF16)
        pk = proj[:, QK_W:2 * QK_W]
        k_ref[...] = pk.astype(BF16)
        pkT = pk.T
        fkT_ref[...] = pkT[0:FOX_W]
        dkT_ref[...] = pkT[FOX_W:QK_W]
        pvT = proj[:, 2 * QK_W:3 * QK_W].T
        fvT_ref[...] = pvT[0:FOX_W]
        dvT_ref[...] = pvT[FOX_W:QK_W]
        pvT_bf = pvT.astype(BF16)
        for t in range(tm // attn_tile):
            vT_ref[t] = pvT_bf[:, t * attn_tile:(t + 1) * attn_tile]
    else:
        q_ref, k_ref, v_ref, u_ref, lf_ref = refs[5:]
        q_ref[...] = proj[:, 0:QK_W]
        k_ref[...] = proj[:, QK_W:2 * QK_W]
        v_ref[...] = proj[:, 2 * QK_W:3 * QK_W]
    u_ref[...] = proj[:, c0:c0 + conv_ch] * _sigmoid(proj[:, c0 + conv_ch:c0 + 2 * conv_ch])
    z = proj[:, c0 + 2 * conv_ch:c0 + 2 * conv_ch + LANES] + bfg_ref[...]
    lane = lax.broadcasted_iota(jnp.int32, z.shape, 1)
    lf = jnp.where(lane < H_FOX, _log_sigmoid(z), 0.0)
    if not transposed:
        lf_ref[...] = lf
        return
    @pl.when(pl.program_id(1) == 0)
    def _():
        carry_ref[...] = jnp.zeros_like(carry_ref)
    hi = lf.astype(BF16)
    r1 = lf - hi.astype(F32)
    mid = r1.astype(BF16)
    lo = (r1 - mid.astype(F32)).astype(BF16)
    tri = tri_ref[...]
    cum = _nn(tri, hi) + _nn(tri, mid) + _nn(tri, lo) + carry_ref[...]
    carry_ref[...] = cum[tm - 1:tm, :]
    cum_ref[...] = cum
    cumT_ref[...] = cum.T[0:SUBLANES, :]
    lfT_ref[...] = lf.T[0:SUBLANES, :]


def _inproj_prompt(x, sc, sh, w1, bfg, tri):
    b, s, d = x.shape
    conv_ch = d - QK_W
    tm = min(ROW_TILE, s)
    ta = min(ATTN_TILE, s)
    nb = s // ta
    row = lambda w: pl.BlockSpec((None, tm, w), lambda i, j: (i, j, 0))
    colT = lambda r: pl.BlockSpec((None, r, tm), lambda i, j: (i, 0, j))
    vec = pl.BlockSpec((None, 1, d), lambda i, j: (i, 0, 0))
    out_shape = (
        jax.ShapeDtypeStruct((b, QK_W, s), BF16),
        jax.ShapeDtypeStruct((b, s, QK_W), BF16),
        jax.ShapeDtypeStruct((b, nb, QK_W, ta), BF16),
        jax.ShapeDtypeStruct((b, FOX_W, s), F32),
        jax.ShapeDtypeStruct((b, FOX_W, s), F32),
        jax.ShapeDtypeStruct((b, DIFF_W, s), F32),
        jax.ShapeDtypeStruct((b, DIFF_W, s), F32),
        jax.ShapeDtypeStruct((b, s, conv_ch), F32),
        jax.ShapeDtypeStruct((b, SUBLANES, s), F32),
        jax.ShapeDtypeStruct((b, s, LANES), F32),
        jax.ShapeDtypeStruct((b, SUBLANES, s), F32),
    )
    out_specs = (colT(QK_W), row(QK_W),
                 pl.BlockSpec((None, tm // ta, QK_W, ta), lambda i, j: (i, j, 0, 0)),
                 colT(FOX_W), colT(FOX_W), colT(DIFF_W), colT(DIFF_W), row(conv_ch),
                 colT(SUBLANES), row(LANES), colT(SUBLANES))
    return pl.pallas_call(
        functools.partial(_inproj_kernel, transposed=True, conv_ch=conv_ch, attn_tile=ta),
        out_shape=out_shape,
        grid=(b, s // tm),
        in_specs=[row(d), vec, vec, _const_spec(w1.shape), _const_spec(bfg.shape), _const_spec(tri.shape)],
        out_specs=out_specs,
        scratch_shapes=[pltpu.VMEM((1, LANES), F32)],
        compiler_params=pltpu.CompilerParams(dimension_semantics=("arbitrary", "arbitrary"),
                                             vmem_limit_bytes=VMEM_LIMIT),
        name="inproj_prompt",
    )(x, sc, sh, w1, bfg, tri)


def _inproj_sample(x, sc, sh, w1, bfg):
    n, d = x.shape
    conv_ch = d - QK_W
    out_shape = (jax.ShapeDtypeStruct((n, QK_W), F32), jax.ShapeDtypeStruct((n, QK_W), F32),
                 jax.ShapeDtypeStruct((n, QK_W), F32), jax.ShapeDtypeStruct((n, conv_ch), F32),
                 jax.ShapeDtypeStruct((n, LANES), F32))
    return pl.pallas_call(
        functools.partial(_inproj_kernel, transposed=False, conv_ch=conv_ch, attn_tile=0),
        out_shape=out_shape,
        compiler_params=pltpu.CompilerParams(vmem_limit_bytes=VMEM_LIMIT),
        name="inproj_sample",
    )(x, sc, sh, w1, bfg)


N_GROUPS = QK_W // LANES
N_FOX_GROUPS = FOX_W // LANES
MAPS_PER_FOX_GROUP = LANES // HEAD_DIM
MAPS_PER_DIFF_GROUP = LANES // D_DIFF
N_MAPS = N_FOX_GROUPS * MAPS_PER_FOX_GROUP + (N_GROUPS - N_FOX_GROUPS) * MAPS_PER_DIFF_GROUP
LOG2E = math.log2(math.e)
SCORE_LOOKAHEAD = 4


def _group_maps(g):
    if g < N_FOX_GROUPS:
        return g * MAPS_PER_FOX_GROUP, MAPS_PER_FOX_GROUP, HEAD_DIM
    first = N_FOX_GROUPS * MAPS_PER_FOX_GROUP + (g - N_FOX_GROUPS) * MAPS_PER_DIFF_GROUP
    return first, MAPS_PER_DIFF_GROUP, D_DIFF


def _prompt_attn_kernel(qT_ref, k_ref, vT_ref, cum_ref, cumT_ref, lam_ref, g_ref, o_ref,
                        qm_scr, m_scr, l_scr, acc_scr, *, lam_init):
    i = pl.program_id(1)
    t = qT_ref.shape[1]
    key_idx = lax.broadcasted_iota(jnp.int32, (t, t), 0)
    qry_idx = lax.broadcasted_iota(jnp.int32, (t, t), 1)
    causal = key_idx <= qry_idx
    row128 = lax.broadcasted_iota(jnp.int32, (LANES, t), 0)

    for g in range(N_GROUPS):
        first, n, width = _group_maps(g)
        qT = qT_ref[g * LANES:(g + 1) * LANES, :].astype(F32)
        for mp in range(n):
            qm_scr[first + mp] = jnp.where((row128 // width) == mp, qT, 0.0).astype(BF16)
    m_scr[...] = jnp.full_like(m_scr, NEG_INF)
    l_scr[...] = jnp.zeros_like(l_scr)
    acc_scr[...] = jnp.zeros_like(acc_scr)

    map_group = [g for g in range(N_GROUPS) for _ in range(_group_maps(g)[1])]

    def step(j, masked):
        start = pl.multiple_of(j * t, t)
        k_tiles, vT_tiles = {}, {}

        def score(mi):
            g = map_group[mi]
            if g not in k_tiles:
                k_tiles[g] = k_ref[pl.ds(start, t), g * LANES:(g + 1) * LANES]
            return _nn(k_tiles[g], qm_scr[mi])

        def update(mi, s):
            g = map_group[mi]
            fox = g < N_FOX_GROUPS
            if g not in vT_tiles:
                vT_tiles[g] = vT_ref[j, g * LANES:(g + 1) * LANES, :]
            c2 = LOG2E if fox else LOG2E * D_DIFF ** -0.5
            if fox:
                s = (s + cumT_ref[mi:mi + 1, :]) - cum_ref[pl.ds(start, t), mi:mi + 1]
            if masked:
                s = jnp.where(causal, s, NEG_INF)
            m_prev = m_scr[mi]
            m_new = jnp.maximum(m_prev, jnp.max(s, axis=0, keepdims=True))
            alpha = jnp.exp2((m_prev - m_new) * c2)
            p = jnp.exp2((s - m_new) * c2)
            l_scr[mi] = alpha * l_scr[mi] + jnp.sum(p, axis=0, keepdims=True)
            acc_scr[mi] = alpha * acc_scr[mi] + _nn(vT_tiles[g], p.astype(BF16))
            m_scr[mi] = m_new

        pending = []
        for idx in range(N_MAPS + SCORE_LOOKAHEAD):
            if idx < N_MAPS:
                pending.append((idx, score(idx)))
            if idx >= SCORE_LOOKAHEAD:
                update(*pending.pop(0))

    def body(j, carry):
        step(j, False)
        return carry

    lax.fori_loop(0, i, body, 0)
    step(i, True)

    def out_T(mi):
        return acc_scr[mi] / l_scr[mi]

    lam = _lambda(lam_ref, lam_init)
    low = row128 < HEAD_DIM
    for g in range(N_GROUPS):
        first, n, _ = _group_maps(g)
        cols = slice(g * LANES, (g + 1) * LANES)
        if g < N_FOX_GROUPS:
            o_ref[:, cols] = jnp.where(low, out_T(first), out_T(first + 1)).T.astype(o_ref.dtype)
        else:
            dT = jnp.where(low, out_T(first) - lam * out_T(first + 1), out_T(first + 2) - lam * out_T(first + 3))
            sq = dT * dT
            ms0 = jnp.mean(sq[0:HEAD_DIM], axis=0, keepdims=True)
            ms1 = jnp.mean(sq[HEAD_DIM:LANES], axis=0, keepdims=True)
            ms = jnp.where(low, ms0, ms1)
            dn = (dT * lax.rsqrt(ms + LN_EPS)).T * g_ref[...] * (1.0 - lam_init)
            o_ref[:, cols] = dn.astype(o_ref.dtype)


def _prompt_attn(qT, k, vT, cum, cumT, lam4, g128, lam_init):
    b, s, _ = k.shape
    t = vT.shape[-1]
    return pl.pallas_call(
        functools.partial(_prompt_attn_kernel, lam_init=lam_init),
        out_shape=jax.ShapeDtypeStruct((b, s, QK_W), BF16),
        grid=(b, s // t),
        in_specs=[pl.BlockSpec((None, QK_W, t), lambda bi, i: (bi, 0, i)),
                  pl.BlockSpec((None, s, QK_W), lambda bi, i: (bi, 0, 0)),
                  pl.BlockSpec((None, s // t, QK_W, t), lambda bi, i: (bi, 0, 0, 0)),
                  pl.BlockSpec((None, s, LANES), lambda bi, i: (bi, 0, 0)),
                  pl.BlockSpec((None, SUBLANES, t), lambda bi, i: (bi, 0, i)),
                  _const_spec(lam4.shape), _const_spec(g128.shape)],
        out_specs=pl.BlockSpec((None, t, QK_W), lambda bi, i: (bi, i, 0)),
        scratch_shapes=[pltpu.VMEM((N_MAPS, LANES, t), BF16),
                        pltpu.VMEM((N_MAPS, 1, t), F32), pltpu.VMEM((N_MAPS, 1, t), F32),
                        pltpu.VMEM((N_MAPS, LANES, t), F32)],
        compiler_params=pltpu.CompilerParams(dimension_semantics=("arbitrary", "arbitrary"),
                                             vmem_limit_bytes=VMEM_LIMIT),
        name="prompt_attn",
    )(qT, k, vT, cum, cumT, lam4, g128)


HALO = 32


def _conv_finish(y, cb_ref, g_ref, b_ref):
    y = _layernorm(y + cb_ref[...], g_ref[...], b_ref[...])
    return y * _sigmoid(y)


def _conv_prompt_kernel(u_ref, halo_ref, cw_ref, cb_ref, g_ref, b_ref, o_ref, win_scr, shift_scr):
    tm = u_ref.shape[0]
    j = pl.program_id(1)
    halo = halo_ref[...]
    win_scr[0:HALO, :] = jnp.where(j == 0, jnp.zeros_like(halo), halo)
    win_scr[HALO:HALO + tm, :] = u_ref[...]
    base = HALO - (CONV_WIDTH - 1)
    acc = None
    for phase in range(SUBLANES):
        taps = [w for w in range(CONV_WIDTH) if (base + w) % SUBLANES == phase]
        span = base + taps[-1] - phase + tm
        shift_scr[0:span, :] = win_scr[phase:phase + span, :]
        for w in taps:
            o = base + w - phase
            term = shift_scr[o:o + tm, :] * cw_ref[w:w + 1, :]
            acc = term if acc is None else acc + term
    o_ref[...] = _conv_finish(acc, cb_ref, g_ref, b_ref).astype(o_ref.dtype)


def _conv_prompt(u, cw, cb, g, bb):
    b, s, c = u.shape
    tm = min(ROW_TILE, s)
    hb = tm // HALO
    return pl.pallas_call(
        _conv_prompt_kernel,
        out_shape=jax.ShapeDtypeStruct((b, s, c), BF16),
        grid=(b, s // tm),
        in_specs=[pl.BlockSpec((None, tm, c), lambda i, j: (i, j, 0)),
                  pl.BlockSpec((None, HALO, c), lambda i, j: (i, jnp.maximum(j * hb - 1, 0), 0)),
                  _const_spec(cw.shape), _const_spec(cb.shape), _const_spec(g.shape), _const_spec(bb.shape)],
        out_specs=pl.BlockSpec((None, tm, c), lambda i, j: (i, j, 0)),
        scratch_shapes=[pltpu.VMEM((HALO + tm, c), F32), pltpu.VMEM((HALO + tm, c), F32)],
        compiler_params=pltpu.CompilerParams(dimension_semantics=("arbitrary", "arbitrary"),
                                             vmem_limit_bytes=VMEM_LIMIT),
        name="conv_prompt",
    )(u, u, cw, cb, g, bb)


def _conv_sample_kernel(buf_ref, u_ref, cw_ref, cb_ref, g_ref, b_ref, o_ref):
    acc = u_ref[...] * cw_ref[CONV_WIDTH - 1:CONV_WIDTH, :]
    for w in range(CONV_WIDTH - 1):
        acc = acc + buf_ref[w] * cw_ref[w:w + 1, :]
    o_ref[...] = _conv_finish(acc, cb_ref, g_ref, b_ref).astype(o_ref.dtype)


def _conv_sample(buf_t, u, cw, cb, g, bb):
    return pl.pallas_call(
        _conv_sample_kernel,
        out_shape=jax.ShapeDtypeStruct(u.shape, BF16),
        name="conv_sample",
    )(buf_t, u, cw, cb, g, bb)


def _post_kernel(x_ref, ao_ref, co_ref, ga_ref, shf_ref, scf_ref, gf_ref, woa_ref, woc_ref,
                 g1_ref, b1_ref, win_ref, wout_ref, g2_ref, b2_ref, o_ref, a_scr, *, alpha, d_ff):
    y = _nn(ao_ref[...], woa_ref[...]) + _nn(co_ref[...], woc_ref[...])
    x1 = _layernorm(alpha * x_ref[...] + (1.0 + ga_ref[...]) * y, g1_ref[...], b1_ref[...])
    h = (x1 * (1.0 + scf_ref[...]) + shf_ref[...]).astype(BF16)
    for c in range(d_ff // FFN_CHUNK):
        lo = c * FFN_CHUNK
        gate = _nn(h, win_ref[:, lo:lo + FFN_CHUNK])
        up = _nn(h, win_ref[:, d_ff + lo:d_ff + lo + FFN_CHUNK])
        a_scr[:, lo:lo + FFN_CHUNK] = (gate * _sigmoid(gate) * up).astype(BF16)
    y2 = _nn(a_scr[...], wout_ref[...])
    o_ref[...] = _layernorm(alpha * x1 + (1.0 + gf_ref[...]) * y2, g2_ref[...], b2_ref[...])


def _post(x, ao, co, mods, woa, woc, g1, b1, w_in, w_out, g2, b2, alpha):
    gdim, r, d = x.shape
    d_ff = w_out.shape[0]
    tm = min(ROW_TILE, r)
    row = lambda w: pl.BlockSpec((None, tm, w), lambda i, j: (i, j, 0))

    def mod_spec(m):
        if m.shape[1] == 1:
            return pl.BlockSpec((None, 1, d), lambda i, j: (i, 0, 0))
        return row(d)

    consts = (woa, woc, g1, b1, w_in, w_out, g2, b2)
    return pl.pallas_call(
        functools.partial(_post_kernel, alpha=alpha, d_ff=d_ff),
        out_shape=jax.ShapeDtypeStruct(x.shape, F32),
        grid=(gdim, r // tm),
        in_specs=[row(d), row(ao.shape[-1]), row(co.shape[-1])] + [mod_spec(m) for m in mods]
                 + [_const_spec(c.shape) for c in consts],
        out_specs=row(d),
        scratch_shapes=[pltpu.VMEM((tm, d_ff), BF16)],
        compiler_params=pltpu.CompilerParams(dimension_semantics=("arbitrary", "arbitrary"),
                                             vmem_limit_bytes=VMEM_LIMIT),
        name="post",
    )(x, ao, co, *mods, *consts)


def _decode_kernel(pt_ref, q_ref, kn_ref, vn_ref, lfn_ref, lam_ref, g_ref, utri_ref,
                   fk_hbm, fv_hbm, lf_hbm, dk_hbm, dv_hbm, o_ref,
                   fk_buf, fv_buf, lf_buf, dk_buf, dv_buf, sem,
                   mf_scr, lsf_scr, accf_scr, md_scr, lsd_scr, accd_scr, carry_scr,
                   *, layer, n_pages, lam_init):
    npg = PAGES_PER_STEP
    b = pl.program_id(0)
    c = pl.program_id(1)
    n_chunks = pl.num_programs(1)
    step = b * n_chunks + c
    n_steps = pl.num_programs(0) * n_chunks
    slot = step % 2
    hbm = (fk_hbm, fv_hbm, lf_hbm, dk_hbm, dv_hbm)
    bufs = (fk_buf, fv_buf, lf_buf, dk_buf, dv_buf)

    def copies(st, sl):
        bb = st // n_chunks
        cc = st % n_chunks
        first = bb * n_pages + (n_chunks - 1 - cc) * npg
        out = []
        for p in range(npg):
            page = pt_ref[first + p]
            for a in range(5):
                out.append(pltpu.make_async_copy(hbm[a].at[layer, page], bufs[a].at[sl, p], sem.at[sl, a]))
        return out

    @pl.when(step == 0)
    def _():
        for cp in copies(step, slot):
            cp.start()

    @pl.when(step + 1 < n_steps)
    def _():
        for cp in copies(step + 1, 1 - slot):
            cp.start()

    row_f = lax.broadcasted_iota(jnp.int32, (SUBLANES, FOX_W), 0)
    lane_f = lax.broadcasted_iota(jnp.int32, (SUBLANES, FOX_W), 1)
    own_f = (lane_f // HEAD_DIM) == row_f
    row_d = lax.broadcasted_iota(jnp.int32, (SUBLANES, DIFF_W), 0)
    lane_d = lax.broadcasted_iota(jnp.int32, (SUBLANES, DIFF_W), 1)
    own_dq = (lane_d // D_DIFF) == row_d
    own_dv = (lane_d // HEAD_DIM) == (row_d // 2)

    q = q_ref[...]
    qf = jnp.where(own_f, q[:, 0:FOX_W], 0.0).astype(BF16)
    qd = jnp.where(own_dq, q[:, FOX_W:QK_W], 0.0).astype(BF16)
    d_scale = D_DIFF ** -0.5

    @pl.when(c == 0)
    def _():
        kn = kn_ref[...].astype(BF16).astype(F32)
        vn = vn_ref[...].astype(BF16).astype(F32)
        mf_scr[...] = jnp.sum(qf.astype(F32) * kn[:, 0:FOX_W], axis=-1, keepdims=True)
        md_scr[...] = jnp.sum(qd.astype(F32) * kn[:, FOX_W:QK_W], axis=-1, keepdims=True) * d_scale
        lsf_scr[...] = jnp.ones_like(lsf_scr)
        lsd_scr[...] = jnp.ones_like(lsd_scr)
        accf_scr[...] = jnp.broadcast_to(vn[:, 0:FOX_W], accf_scr.shape)
        accd_scr[...] = jnp.broadcast_to(vn[:, FOX_W:QK_W], accd_scr.shape)
        carry_scr[...] = lfn_ref[...]

    for cp in copies(step, slot):
        cp.wait()

    utri = utri_ref[...]
    carry = carry_scr[...]
    bias = [None] * npg
    for p in reversed(range(npg)):
        lf = lf_buf[slot, p]
        hi = lf.astype(BF16)
        r1 = lf - hi.astype(F32)
        mid = r1.astype(BF16)
        lo = (r1 - mid.astype(F32)).astype(BF16)
        bias[p] = _nn(hi, utri) + _nn(mid, utri) + _nn(lo, utri) + carry
        carry = carry + jnp.sum(lf, axis=-1, keepdims=True)
    carry_scr[...] = carry

    sf = [_nn(qf, fk_buf[slot, p].astype(BF16)) + bias[p] for p in range(npg)]
    sd = [_nn(qd, dk_buf[slot, p].astype(BF16)) * d_scale for p in range(npg)]

    def online(s_list, m_scr, l_scr, acc_scr, v_buf):
        m_prev = m_scr[...]
        m_new = m_prev
        for s in s_list:
            m_new = jnp.maximum(m_new, jnp.max(s, axis=-1, keepdims=True))
        alpha = jnp.exp(m_prev - m_new)
        l_new = alpha * l_scr[...]
        acc = alpha * acc_scr[...]
        for p, s in enumerate(s_list):
            e = jnp.exp(s - m_new)
            l_new = l_new + jnp.sum(e, axis=-1, keepdims=True)
            acc = acc + _nt(e.astype(BF16), v_buf[slot, p].astype(BF16))
        m_scr[...] = m_new
        l_scr[...] = l_new
        acc_scr[...] = acc

    online(sf, mf_scr, lsf_scr, accf_scr, fv_buf)
    online(sd, md_scr, lsd_scr, accd_scr, dv_buf)

    @pl.when(c == n_chunks - 1)
    def _():
        zf = accf_scr[...] / lsf_scr[...]
        fox = jnp.sum(jnp.where(own_f, zf, 0.0), axis=0, keepdims=True)
        lam = _lambda(lam_ref, lam_init)
        coef = jnp.where((row_d % 2) == 0, 1.0, -lam)
        zd = coef * (accd_scr[...] / lsd_scr[...])
        dif = jnp.sum(jnp.where(own_dv, zd, 0.0), axis=0, keepdims=True)
        sq = dif * dif
        lane1 = lax.broadcasted_iota(jnp.int32, (1, DIFF_W), 1)
        ms = jnp.zeros_like(dif)
        for hd in range(H_DIFF):
            in_head = (lane1 // HEAD_DIM) == hd
            ms_h = jnp.sum(jnp.where(in_head, sq, 0.0), axis=-1, keepdims=True) / HEAD_DIM
            ms = jnp.where(in_head, ms_h, ms)
        dn = dif * lax.rsqrt(ms + LN_EPS) * g_ref[...] * (1.0 - lam_init)
        o_ref[:, 0:FOX_W] = fox.astype(o_ref.dtype)
        o_ref[:, FOX_W:QK_W] = dn.astype(o_ref.dtype)


def _decode_attn(page_table, q, kn, vn, lfn, lam4, g256, utri, fkT, fvT, lfT, dkT, dvT, layer, lam_init):
    nb, n_pages = page_table.shape
    npg = PAGES_PER_STEP
    n_chunks = n_pages // npg
    page = fkT.shape[-1]
    per_b = lambda w: pl.BlockSpec((None, 1, w), lambda b, c, pt: (b, 0, 0))
    any_spec = pl.BlockSpec(memory_space=pl.ANY)
    grid_spec = pltpu.PrefetchScalarGridSpec(
        num_scalar_prefetch=1,
        grid=(nb, n_chunks),
        in_specs=[per_b(QK_W), per_b(QK_W), per_b(QK_W),
                  pl.BlockSpec((None, SUBLANES, LANES), lambda b, c, pt: (b, 0, 0)),
                  pl.BlockSpec(lam4.shape, lambda b, c, pt: (0, 0)),
                  pl.BlockSpec(g256.shape, lambda b, c, pt: (0, 0)),
                  pl.BlockSpec(utri.shape, lambda b, c, pt: (0, 0)),
                  any_spec, any_spec, any_spec, any_spec, any_spec],
        out_specs=per_b(QK_W),
        scratch_shapes=[pltpu.VMEM((2, npg, FOX_W, page), F32), pltpu.VMEM((2, npg, FOX_W, page), F32),
                        pltpu.VMEM((2, npg, SUBLANES, page), F32),
                        pltpu.VMEM((2, npg, DIFF_W, page), F32), pltpu.VMEM((2, npg, DIFF_W, page), F32),
                        pltpu.SemaphoreType.DMA((2, 5)),
                        pltpu.VMEM((SUBLANES, 1), F32), pltpu.VMEM((SUBLANES, 1), F32),
                        pltpu.VMEM((SUBLANES, FOX_W), F32),
                        pltpu.VMEM((SUBLANES, 1), F32), pltpu.VMEM((SUBLANES, 1), F32),
                        pltpu.VMEM((SUBLANES, DIFF_W), F32),
                        pltpu.VMEM((SUBLANES, LANES), F32)],
    )
    return pl.pallas_call(
        functools.partial(_decode_kernel, layer=layer, n_pages=n_pages, lam_init=lam_init),
        out_shape=jax.ShapeDtypeStruct((nb, 1, QK_W), BF16),
        grid_spec=grid_spec,
        compiler_params=pltpu.CompilerParams(dimension_semantics=("arbitrary", "arbitrary"),
                                             vmem_limit_bytes=VMEM_LIMIT),
        name="decode_attn",
    )(page_table.reshape(-1), q, kn, vn, lfn, lam4, g256, utri, fkT, fvT, lfT, dkT, dvT)


def _pack_w_in(w):
    o = 0
    parts = {}
    for name, width in (("fq", FOX_W), ("fk", FOX_W), ("fv", FOX_W), ("fg", H_FOX), ("dq", DIFF_W),
                        ("dk", DIFF_W), ("dv", DIFF_W)):
        parts[name] = w[:, o:o + width]
        o += width
    conv = w[:, o:]
    fg = jnp.pad(parts["fg"], ((0, 0), (0, LANES - H_FOX)))
    packed = jnp.concatenate([parts["fq"] * (HEAD_DIM ** -0.5), parts["dq"], parts["fk"], parts["dk"],
                              parts["fv"], parts["dv"], conv, fg], axis=1)
    return packed.astype(BF16)


def kernel(x_prompt, x_sample, cache_fox_k, cache_fox_v, cache_fox_logf, cache_diff_k, cache_diff_v, state_conv, page_table, c_prompt, c_sample, w_ada, b_ada, w_in, b_fgate, lambda_q1, lambda_k1, lambda_q2, lambda_k2, diff_norm_g, conv_w, conv_b, conv_ln_g, conv_ln_b, w_o, ln_mix_g, ln_mix_b, w_ffn_in, w_ffn_out, ln_ffn_g, ln_ffn_b):
    depth = w_in.shape[0]
    b, s, d = x_prompt.shape
    nb = x_sample.shape[0]
    conv_ch = d - QK_W
    n_pool, page = cache_fox_k.shape[1:3]
    alpha = (2.0 * depth) ** 0.25

    fkT = jnp.transpose(cache_fox_k, (0, 1, 3, 4, 2)).reshape(depth, n_pool, FOX_W, page)
    fvT = jnp.transpose(cache_fox_v, (0, 1, 3, 4, 2)).reshape(depth, n_pool, FOX_W, page)
    dkT = jnp.transpose(cache_diff_k, (0, 1, 3, 4, 2)).reshape(depth, n_pool, DIFF_W, page)
    dvT = jnp.transpose(cache_diff_v, (0, 1, 3, 4, 2)).reshape(depth, n_pool, DIFF_W, page)
    lfT = jnp.pad(jnp.transpose(cache_fox_logf, (0, 1, 3, 2)), ((0, 0), (0, 0), (0, SUBLANES - H_FOX), (0, 0)))
    conv_hist = jnp.transpose(state_conv, (0, 2, 1, 3))

    ada = _ada(jnp.concatenate([c_prompt, c_sample], axis=0), w_ada.astype(BF16), b_ada)

    tm = min(ROW_TILE, s)
    tri = jnp.tril(jnp.ones((tm, tm), F32)).astype(BF16)
    utri = jnp.triu(jnp.ones((page, page), F32), 1).T.astype(BF16)
    xp, xs = x_prompt, x_sample.reshape(nb, d)
    new_p = [[] for _ in range(6)]
    new_s = [[] for _ in range(6)]
    for l in range(depth):
        lam_init = 0.8 - 0.6 * math.exp(-0.3 * l)
        w1 = _pack_w_in(w_in[l])
        bfg = jnp.pad(b_fgate[l], (0, LANES - H_FOX)).reshape(1, LANES)
        lam4 = jnp.stack([lambda_q1[l], lambda_k1[l], lambda_q2[l], lambda_k2[l]])
        g128 = jnp.tile(diff_norm_g[l], 2).reshape(1, LANES)
        g256 = jnp.tile(diff_norm_g[l], H_DIFF).reshape(1, DIFF_W)
        cw, cb = conv_w[l], conv_b[l].reshape(1, conv_ch)
        cg, cbb = conv_ln_g[l].reshape(1, conv_ch), conv_ln_b[l].reshape(1, conv_ch)
        wo = w_o[l].astype(BF16)
        woa, woc = wo[0:QK_W], wo[QK_W:]
        wfi, wfo = w_ffn_in[l].astype(BF16), w_ffn_out[l].astype(BF16)
        g1, b1 = ln_mix_g[l].reshape(1, d), ln_mix_b[l].reshape(1, d)
        g2, b2 = ln_ffn_g[l].reshape(1, d), ln_ffn_b[l].reshape(1, d)
        mods = [ada[l, :, i * d:(i + 1) * d] for i in range(6)]
        mp = [m[0:b].reshape(b, 1, d) for m in mods]
        ms = [m[b:b + nb].reshape(1, nb, d) for m in mods]

        (qT, kr, vT, fk_t, fv_t, dk_t, dv_t, u, lf_t, cum, cumT) = _inproj_prompt(xp, mp[1], mp[0], w1, bfg, tri)
        ao = _prompt_attn(qT, kr, vT, cum, cumT, lam4, g128, lam_init)
        co = _conv_prompt(u, cw, cb, cg, cbb)
        xp = _post(xp, ao, co, (mp[2], mp[3], mp[4], mp[5]), woa, woc, g1, b1, wfi, wfo, g2, b2, alpha)
        unT = lambda a, h: jnp.transpose(a.reshape(b, h, HEAD_DIM, s), (0, 3, 1, 2))
        new_p[0].append(unT(fk_t, H_FOX))
        new_p[1].append(unT(fv_t, H_FOX))
        new_p[2].append(jnp.transpose(lf_t[:, 0:H_FOX, :], (0, 2, 1)))
        new_p[3].append(unT(dk_t, H_DIFF))
        new_p[4].append(unT(dv_t, H_DIFF))
        new_p[5].append(u[:, s - (CONV_WIDTH - 1):, :])

        qs, ks, vs, us, lfs = _inproj_sample(xs, ms[1][0], ms[0][0], w1, bfg)
        lfn = jnp.broadcast_to(jnp.pad(lfs[:, 0:H_FOX], ((0, 0), (0, SUBLANES - H_FOX)))[:, :, None],
                               (nb, SUBLANES, LANES))
        r3 = lambda a: a.reshape(nb, 1, a.shape[-1])
        aos = _decode_attn(page_table, r3(qs), r3(ks), r3(vs), lfn, lam4, g256, utri,
                           fkT, fvT, lfT, dkT, dvT, l, lam_init)
        cos = _conv_sample(conv_hist[l], us, cw, cb, cg, cbb)
        xs = _post(xs.reshape(1, nb, d), aos.reshape(1, nb, QK_W), cos.reshape(1, nb, conv_ch),
                   (ms[2], ms[3], ms[4], ms[5]), woa, woc, g1, b1, wfi, wfo, g2, b2, alpha).reshape(nb, d)
        new_s[0].append(ks[:, 0:FOX_W].reshape(nb, 1, H_FOX, HEAD_DIM))
        new_s[1].append(vs[:, 0:FOX_W].reshape(nb, 1, H_FOX, HEAD_DIM))
        new_s[2].append(lfs[:, 0:H_FOX].reshape(nb, 1, H_FOX))
        new_s[3].append(ks[:, FOX_W:QK_W].reshape(nb, 1, H_DIFF, HEAD_DIM))
        new_s[4].append(vs[:, FOX_W:QK_W].reshape(nb, 1, H_DIFF, HEAD_DIM))
        new_s[5].append(jnp.concatenate([state_conv[l][:, 1:, :], us[:, None, :]], axis=1))

    return (xp, xs.reshape(nb, 1, d),
            jnp.stack(new_p[0]), jnp.stack(new_p[1]), jnp.stack(new_p[2]),
            jnp.stack(new_p[3]), jnp.stack(new_p[4]), jnp.stack(new_p[5]),
            jnp.stack(new_s[0]), jnp.stack(new_s[1]), jnp.stack(new_s[2]),
            jnp.stack(new_s[3]), jnp.stack(new_s[4]), jnp.stack(new_s[5]))
```

```python
import functools
import math

import jax
import jax.numpy as jnp
from jax import lax
from jax.experimental import pallas as pl
from jax.experimental.pallas import tpu as pltpu

F32 = jnp.float32
BF16 = jnp.bfloat16

HEAD_DIM = 64
H_FOX = 6
FOX_W = H_FOX * HEAD_DIM
H_DIFF = 4
D_DIFF = HEAD_DIM // 2
DIFF_W = H_DIFF * HEAD_DIM
QK_W = FOX_W + DIFF_W
CONV_WIDTH = 31
LN_EPS = 1e-5
NEG_INF = -1e30
LANES = 128
SUBLANES = 8
VMEM_LIMIT = 56 * 1024 * 1024

ROW_TILE = 512
ATTN_TILE = 512
FFN_CHUNK = 256
PAGES_PER_STEP = 16


def _nn(a, b):
    return jnp.dot(a, b, preferred_element_type=F32)


def _nt(a, b):
    return lax.dot_general(a, b, (((1,), (1,)), ((), ())), preferred_element_type=F32)


def _sigmoid(x):
    return jax.nn.sigmoid(x)


def _log_sigmoid(x):
    return jnp.minimum(x, 0.0) - jnp.log1p(jnp.exp(-jnp.abs(x)))


def _layernorm(x, g, b):
    mu = jnp.mean(x, axis=-1, keepdims=True)
    var = jnp.mean(jnp.square(x - mu), axis=-1, keepdims=True)
    return (x - mu) * lax.rsqrt(var + LN_EPS) * g + b


def _lambda(lam_ref, lam_init):
    a = lam_ref[...]
    s1 = jnp.sum(a[0:1] * a[1:2], axis=-1, keepdims=True)
    s2 = jnp.sum(a[2:3] * a[3:4], axis=-1, keepdims=True)
    return jnp.exp(s1) - jnp.exp(s2) + lam_init


def _const_spec(shape):
    nd = len(shape)
    return pl.BlockSpec(shape, lambda *_: (0,) * nd, pipeline_mode=pl.Buffered(1))


def _ada_kernel(c_ref, w_ref, b_ref, o_ref):
    c = c_ref[...]
    a = (c * _sigmoid(c)).astype(BF16)
    o_ref[...] = _nn(a, w_ref[...]) + b_ref[...]


def _ada(c_all, w_ada_bf, b_ada):
    depth, d, n = w_ada_bf.shape
    rows = c_all.shape[0]
    tn = d
    return pl.pallas_call(
        _ada_kernel,
        out_shape=jax.ShapeDtypeStruct((depth, rows, n), F32),
        grid=(depth, n // tn),
        in_specs=[pl.BlockSpec((rows, d), lambda l, j: (0, 0)),
                  pl.BlockSpec((None, d, tn), lambda l, j: (l, 0, j)),
                  pl.BlockSpec((None, 1, tn), lambda l, j: (l, 0, j))],
        out_specs=pl.BlockSpec((None, rows, tn), lambda l, j: (l, 0, j)),
        compiler_params=pltpu.CompilerParams(dimension_semantics=("arbitrary", "arbitrary"),
                                             vmem_limit_bytes=VMEM_LIMIT),
        name="ada",
    )(c_all, w_ada_bf, b_ada.reshape(depth, 1, n))


def _inproj_kernel(*refs, transposed, conv_ch, attn_tile, n_prev=0):
    x_ref, sc_ref, sh_ref, w_ref, bfg_ref = refs[:5]
    h = (x_ref[...] * (1.0 + sc_ref[...]) + sh_ref[...]).astype(BF16)
    c0 = 3 * QK_W
    proj = _nn(h, w_ref[...])
    if transposed:
        n_in = 6 + (4 if n_prev else 0)
        tri_ref, prev_refs = refs[5], refs[6:n_in]
        (qT_ref, k_ref, vT_ref, fkT_ref, fvT_ref, dkT_ref, dvT_ref, u_ref, lfT_ref,
         cum_ref, cumT_ref, carry_ref) = refs[n_in:]
        tm = x_ref.shape[0]

        def put_state(o_ref, which, val):
            if n_prev:
                o_ref[0:n_prev] = prev_refs[which][...]
            o_ref[n_prev] = val

        qT_ref[...] = proj[:, 0:QK_W].T.astype(BF16)
        pk = proj[:, QK_W:2 * QK_W]
        k_ref[...] = pk.astype(BF16)
        pkT = pk.T
        put_state(fkT_ref, 0, pkT[0:FOX_W])
        put_state(dkT_ref, 2, pkT[FOX_W:QK_W])
        pvT = proj[:, 2 * QK_W:3 * QK_W].T
        put_state(fvT_ref, 1, pvT[0:FOX_W])
        put_state(dvT_ref, 3, pvT[FOX_W:QK_W])
        pvT_bf = pvT.astype(BF16)
        for t in range(tm // attn_tile):
            vT_ref[t] = pvT_bf[:, t * attn_tile:(t + 1) * attn_tile]
    else:
        q_ref, k_ref, v_ref, u_ref, lf_ref = refs[5:]
        q_ref[...] = proj[:, 0:QK_W]
        k_ref[...] = proj[:, QK_W:2 * QK_W]
        v_ref[...] = proj[:, 2 * QK_W:3 * QK_W]
    u_ref[...] = proj[:, c0:c0 + conv_ch] * _sigmoid(proj[:, c0 + conv_ch:c0 + 2 * conv_ch])
    z = proj[:, c0 + 2 * conv_ch:c0 + 2 * conv_ch + LANES] + bfg_ref[...]
    lane = lax.broadcasted_iota(jnp.int32, z.shape, 1)
    lf = jnp.where(lane < H_FOX, _log_sigmoid(z), 0.0)
    if not transposed:
        lf_ref[...] = lf
        return
    @pl.when(pl.program_id(1) == 0)
    def _():
        carry_ref[...] = jnp.zeros_like(carry_ref)
    hi = lf.astype(BF16)
    r1 = lf - hi.astype(F32)
    mid = r1.astype(BF16)
    lo = (r1 - mid.astype(F32)).astype(BF16)
    tri = tri_ref[...]
    cum = _nn(tri, hi) + _nn(tri, mid) + _nn(tri, lo) + carry_ref[...]
    carry_ref[...] = cum[tm - 1:tm, :]
    cum_ref[...] = cum
    cumT_ref[...] = cum.T[0:SUBLANES, :]
    lfT_ref[...] = lf.T[0:SUBLANES, :]


def _inproj_prompt(x, sc, sh, w1, bfg, tri, prev):
    b, s, d = x.shape
    conv_ch = d - QK_W
    tm = min(ROW_TILE, s)
    ta = min(ATTN_TILE, s)
    nb = s // ta
    n_prev = 0 if prev is None else prev[0].shape[0]
    row = lambda w: pl.BlockSpec((None, tm, w), lambda i, j: (i, j, 0))
    colT = lambda r: pl.BlockSpec((None, r, tm), lambda i, j: (i, 0, j))
    stack = lambda n, r: pl.BlockSpec((n, None, r, tm), lambda i, j: (0, i, 0, j))
    vec = pl.BlockSpec((None, 1, d), lambda i, j: (i, 0, 0))
    state_widths = (FOX_W, FOX_W, DIFF_W, DIFF_W)
    out_shape = (
        jax.ShapeDtypeStruct((b, QK_W, s), BF16),
        jax.ShapeDtypeStruct((b, s, QK_W), BF16),
        jax.ShapeDtypeStruct((b, nb, QK_W, ta), BF16),
    ) + tuple(jax.ShapeDtypeStruct((n_prev + 1, b, r, s), F32) for r in state_widths) + (
        jax.ShapeDtypeStruct((b, s, conv_ch), F32),
        jax.ShapeDtypeStruct((b, SUBLANES, s), F32),
        jax.ShapeDtypeStruct((b, s, LANES), F32),
        jax.ShapeDtypeStruct((b, SUBLANES, s), F32),
    )
    out_specs = (colT(QK_W), row(QK_W),
                 pl.BlockSpec((None, tm // ta, QK_W, ta), lambda i, j: (i, j, 0, 0)),
                 ) + tuple(stack(n_prev + 1, r) for r in state_widths) + (row(conv_ch),
                 colT(SUBLANES), row(LANES), colT(SUBLANES))
    prev = () if prev is None else tuple(prev)
    return pl.pallas_call(
        functools.partial(_inproj_kernel, transposed=True, conv_ch=conv_ch, attn_tile=ta, n_prev=n_prev),
        out_shape=out_shape,
        grid=(b, s // tm),
        in_specs=[row(d), vec, vec, _const_spec(w1.shape), _const_spec(bfg.shape), _const_spec(tri.shape)]
                 + [stack(n_prev, r) for r, _ in zip(state_widths, prev)],
        out_specs=out_specs,
        scratch_shapes=[pltpu.VMEM((1, LANES), F32)],
        compiler_params=pltpu.CompilerParams(dimension_semantics=("arbitrary", "arbitrary"),
                                             vmem_limit_bytes=VMEM_LIMIT),
        name="inproj_prompt",
    )(x, sc, sh, w1, bfg, tri, *prev)


def _inproj_sample(x, sc, sh, w1, bfg):
    n, d = x.shape
    conv_ch = d - QK_W
    out_shape = (jax.ShapeDtypeStruct((n, QK_W), F32), jax.ShapeDtypeStruct((n, QK_W), F32),
                 jax.ShapeDtypeStruct((n, QK_W), F32), jax.ShapeDtypeStruct((n, conv_ch), F32),
                 jax.ShapeDtypeStruct((n, LANES), F32))
    return pl.pallas_call(
        functools.partial(_inproj_kernel, transposed=False, conv_ch=conv_ch, attn_tile=0),
        out_shape=out_shape,
        compiler_params=pltpu.CompilerParams(vmem_limit_bytes=VMEM_LIMIT),
        name="inproj_sample",
    )(x, sc, sh, w1, bfg)


N_GROUPS = QK_W // LANES
N_FOX_GROUPS = FOX_W // LANES
MAPS_PER_FOX_GROUP = LANES // HEAD_DIM
MAPS_PER_DIFF_GROUP = LANES // D_DIFF
N_MAPS = N_FOX_GROUPS * MAPS_PER_FOX_GROUP + (N_GROUPS - N_FOX_GROUPS) * MAPS_PER_DIFF_GROUP
LOG2E = math.log2(math.e)
SCORE_LOOKAHEAD = 4


def _group_maps(g):
    if g < N_FOX_GROUPS:
        return g * MAPS_PER_FOX_GROUP, MAPS_PER_FOX_GROUP, HEAD_DIM
    first = N_FOX_GROUPS * MAPS_PER_FOX_GROUP + (g - N_FOX_GROUPS) * MAPS_PER_DIFF_GROUP
    return first, MAPS_PER_DIFF_GROUP, D_DIFF


def _prompt_attn_kernel(qT_ref, k_ref, vT_ref, cum_ref, cumT_ref, lam_ref, g_ref, o_ref,
                        qm_scr, m_scr, l_scr, acc_scr, *, lam_init):
    i = pl.program_id(1)
    t = qT_ref.shape[1]
    key_idx = lax.broadcasted_iota(jnp.int32, (t, t), 0)
    qry_idx = lax.broadcasted_iota(jnp.int32, (t, t), 1)
    causal = key_idx <= qry_idx
    row128 = lax.broadcasted_iota(jnp.int32, (LANES, t), 0)

    for g in range(N_GROUPS):
        first, n, width = _group_maps(g)
        qT = qT_ref[g * LANES:(g + 1) * LANES, :].astype(F32)
        for mp in range(n):
            qm_scr[first + mp] = jnp.where((row128 // width) == mp, qT, 0.0).astype(BF16)
    m_scr[...] = jnp.full_like(m_scr, NEG_INF)
    l_scr[...] = jnp.zeros_like(l_scr)
    acc_scr[...] = jnp.zeros_like(acc_scr)

    map_group = [g for g in range(N_GROUPS) for _ in range(_group_maps(g)[1])]

    def step(j, masked):
        start = pl.multiple_of(j * t, t)
        k_tiles, vT_tiles = {}, {}

        def score(mi):
            g = map_group[mi]
            if g not in k_tiles:
                k_tiles[g] = k_ref[pl.ds(start, t), g * LANES:(g + 1) * LANES]
            return _nn(k_tiles[g], qm_scr[mi])

        def update(mi, s):
            g = map_group[mi]
            fox = g < N_FOX_GROUPS
            if g not in vT_tiles:
                vT_tiles[g] = vT_ref[j, g * LANES:(g + 1) * LANES, :]
            c2 = LOG2E if fox else LOG2E * D_DIFF ** -0.5
            if fox:
                s = (s + cumT_ref[mi:mi + 1, :]) - cum_ref[pl.ds(start, t), mi:mi + 1]
            if masked:
                s = jnp.where(causal, s, NEG_INF)
            m_prev = m_scr[mi]
            m_new = jnp.maximum(m_prev, jnp.max(s, axis=0, keepdims=True))
            alpha = jnp.exp2((m_prev - m_new) * c2)
            p = jnp.exp2((s - m_new) * c2)
            l_scr[mi] = alpha * l_scr[mi] + jnp.sum(p, axis=0, keepdims=True)
            acc_scr[mi] = alpha * acc_scr[mi] + _nn(vT_tiles[g], p.astype(BF16))
            m_scr[mi] = m_new

        pending = []
        for idx in range(N_MAPS + SCORE_LOOKAHEAD):
            if idx < N_MAPS:
                pending.append((idx, score(idx)))
            if idx >= SCORE_LOOKAHEAD:
                update(*pending.pop(0))

    def body(j, carry):
        step(j, False)
        return carry

    lax.fori_loop(0, i, body, 0)
    step(i, True)

    def out_T(mi):
        return acc_scr[mi] / l_scr[mi]

    lam = _lambda(lam_ref, lam_init)
    low = row128 < HEAD_DIM
    for g in range(N_GROUPS):
        first, n, _ = _group_maps(g)
        cols = slice(g * LANES, (g + 1) * LANES)
        if g < N_FOX_GROUPS:
            o_ref[:, cols] = jnp.where(low, out_T(first), out_T(first + 1)).T.astype(o_ref.dtype)
        else:
            dT = jnp.where(low, out_T(first) - lam * out_T(first + 1), out_T(first + 2) - lam * out_T(first + 3))
            sq = dT * dT
            ms0 = jnp.mean(sq[0:HEAD_DIM], axis=0, keepdims=True)
            ms1 = jnp.mean(sq[HEAD_DIM:LANES], axis=0, keepdims=True)
            ms = jnp.where(low, ms0, ms1)
            dn = (dT * lax.rsqrt(ms + LN_EPS)).T * g_ref[...] * (1.0 - lam_init)
            o_ref[:, cols] = dn.astype(o_ref.dtype)


def _prompt_attn(qT, k, vT, cum, cumT, lam4, g128, lam_init):
    b, s, _ = k.shape
    t = vT.shape[-1]
    return pl.pallas_call(
        functools.partial(_prompt_attn_kernel, lam_init=lam_init),
        out_shape=jax.ShapeDtypeStruct((b, s, QK_W), BF16),
        grid=(b, s // t),
        in_specs=[pl.BlockSpec((None, QK_W, t), lambda bi, i: (bi, 0, i)),
                  pl.BlockSpec((None, s, QK_W), lambda bi, i: (bi, 0, 0)),
                  pl.BlockSpec((None, s // t, QK_W, t), lambda bi, i: (bi, 0, 0, 0)),
                  pl.BlockSpec((None, s, LANES), lambda bi, i: (bi, 0, 0)),
                  pl.BlockSpec((None, SUBLANES, t), lambda bi, i: (bi, 0, i)),
                  _const_spec(lam4.shape), _const_spec(g128.shape)],
        out_specs=pl.BlockSpec((None, t, QK_W), lambda bi, i: (bi, i, 0)),
        scratch_shapes=[pltpu.VMEM((N_MAPS, LANES, t), BF16),
                        pltpu.VMEM((N_MAPS, 1, t), F32), pltpu.VMEM((N_MAPS, 1, t), F32),
                        pltpu.VMEM((N_MAPS, LANES, t), F32)],
        compiler_params=pltpu.CompilerParams(dimension_semantics=("arbitrary", "arbitrary"),
                                             vmem_limit_bytes=VMEM_LIMIT),
        name="prompt_attn",
    )(qT, k, vT, cum, cumT, lam4, g128)


HALO = 32


def _conv_finish(y, cb_ref, g_ref, b_ref):
    y = _layernorm(y + cb_ref[...], g_ref[...], b_ref[...])
    return y * _sigmoid(y)


def _conv_prompt_kernel(u_ref, halo_ref, cw_ref, cb_ref, g_ref, b_ref, o_ref, win_scr, shift_scr):
    tm = u_ref.shape[0]
    j = pl.program_id(1)
    halo = halo_ref[...]
    win_scr[0:HALO, :] = jnp.where(j == 0, jnp.zeros_like(halo), halo)
    win_scr[HALO:HALO + tm, :] = u_ref[...]
    base = HALO - (CONV_WIDTH - 1)
    acc = None
    for phase in range(SUBLANES):
        taps = [w for w in range(CONV_WIDTH) if (base + w) % SUBLANES == phase]
        span = base + taps[-1] - phase + tm
        shift_scr[0:span, :] = win_scr[phase:phase + span, :]
        for w in taps:
            o = base + w - phase
            term = shift_scr[o:o + tm, :] * cw_ref[w:w + 1, :]
            acc = term if acc is None else acc + term
    o_ref[...] = _conv_finish(acc, cb_ref, g_ref, b_ref).astype(o_ref.dtype)


def _conv_prompt(u, cw, cb, g, bb):
    b, s, c = u.shape
    tm = min(ROW_TILE, s)
    hb = tm // HALO
    return pl.pallas_call(
        _conv_prompt_kernel,
        out_shape=jax.ShapeDtypeStruct((b, s, c), BF16),
        grid=(b, s // tm),
        in_specs=[pl.BlockSpec((None, tm, c), lambda i, j: (i, j, 0)),
                  pl.BlockSpec((None, HALO, c), lambda i, j: (i, jnp.maximum(j * hb - 1, 0), 0)),
                  _const_spec(cw.shape), _const_spec(cb.shape), _const_spec(g.shape), _const_spec(bb.shape)],
        out_specs=pl.BlockSpec((None, tm, c), lambda i, j: (i, j, 0)),
        scratch_shapes=[pltpu.VMEM((HALO + tm, c), F32), pltpu.VMEM((HALO + tm, c), F32)],
        compiler_params=pltpu.CompilerParams(dimension_semantics=("arbitrary", "arbitrary"),
                                             vmem_limit_bytes=VMEM_LIMIT),
        name="conv_prompt",
    )(u, u, cw, cb, g, bb)


def _conv_sample_kernel(buf_ref, u_ref, cw_ref, cb_ref, g_ref, b_ref, o_ref):
    acc = u_ref[...] * cw_ref[CONV_WIDTH - 1:CONV_WIDTH, :]
    for w in range(CONV_WIDTH - 1):
        acc = acc + buf_ref[w] * cw_ref[w:w + 1, :]
    o_ref[...] = _conv_finish(acc, cb_ref, g_ref, b_ref).astype(o_ref.dtype)


def _conv_sample(buf_t, u, cw, cb, g, bb):
    return pl.pallas_call(
        _conv_sample_kernel,
        out_shape=jax.ShapeDtypeStruct(u.shape, BF16),
        name="conv_sample",
    )(buf_t, u, cw, cb, g, bb)


def _post_kernel(x_ref, ao_ref, co_ref, ga_ref, shf_ref, scf_ref, gf_ref, woa_ref, woc_ref,
                 g1_ref, b1_ref, win_ref, wout_ref, g2_ref, b2_ref, o_ref, a_scr, *, alpha, d_ff):
    y = _nn(ao_ref[...], woa_ref[...]) + _nn(co_ref[...], woc_ref[...])
    x1 = _layernorm(alpha * x_ref[...] + (1.0 + ga_ref[...]) * y, g1_ref[...], b1_ref[...])
    h = (x1 * (1.0 + scf_ref[...]) + shf_ref[...]).astype(BF16)
    for c in range(d_ff // FFN_CHUNK):
        lo = c * FFN_CHUNK
        gate = _nn(h, win_ref[:, lo:lo + FFN_CHUNK])
        up = _nn(h, win_ref[:, d_ff + lo:d_ff + lo + FFN_CHUNK])
        a_scr[:, lo:lo + FFN_CHUNK] = (gate * _sigmoid(gate) * up).astype(BF16)
    y2 = _nn(a_scr[...], wout_ref[...])
    o_ref[...] = _layernorm(alpha * x1 + (1.0 + gf_ref[...]) * y2, g2_ref[...], b2_ref[...])


def _post(x, ao, co, mods, woa, woc, g1, b1, w_in, w_out, g2, b2, alpha):
    gdim, r, d = x.shape
    d_ff = w_out.shape[0]
    tm = min(ROW_TILE, r)
    row = lambda w: pl.BlockSpec((None, tm, w), lambda i, j: (i, j, 0))

    def mod_spec(m):
        if m.shape[1] == 1:
            return pl.BlockSpec((None, 1, d), lambda i, j: (i, 0, 0))
        return row(d)

    consts = (woa, woc, g1, b1, w_in, w_out, g2, b2)
    return pl.pallas_call(
        functools.partial(_post_kernel, alpha=alpha, d_ff=d_ff),
        out_shape=jax.ShapeDtypeStruct(x.shape, F32),
        grid=(gdim, r // tm),
        in_specs=[row(d), row(ao.shape[-1]), row(co.shape[-1])] + [mod_spec(m) for m in mods]
                 + [_const_spec(c.shape) for c in consts],
        out_specs=row(d),
        scratch_shapes=[pltpu.VMEM((tm, d_ff), BF16)],
        compiler_params=pltpu.CompilerParams(dimension_semantics=("arbitrary", "arbitrary"),
                                             vmem_limit_bytes=VMEM_LIMIT),
        name="post",
    )(x, ao, co, *mods, *consts)


def _decode_ops(pt_ref, q_ref, kn_ref, vn_ref, lfn_ref, lam_ref, g_ref, utri_ref,
                fk_hbm, fv_hbm, lf_hbm, dk_hbm, dv_hbm, o_ref,
                fk_buf, fv_buf, lf_buf, dk_buf, dv_buf, sem,
                mf_scr, lsf_scr, accf_scr, md_scr, lsd_scr, accd_scr, carry_scr,
                *, layer, n_pages, lam_init):
    npg = PAGES_PER_STEP
    n_chunks = n_pages // npg
    hbm = (fk_hbm, fv_hbm, lf_hbm, dk_hbm, dv_hbm)
    bufs = (fk_buf, fv_buf, lf_buf, dk_buf, dv_buf)

    def copies(ch, sl):
        bb = ch // n_chunks
        cc = ch % n_chunks
        first = bb * n_pages + (n_chunks - 1 - cc) * npg
        out = []
        for p in range(npg):
            page = pt_ref[first + p]
            for a in range(5):
                out.append(pltpu.make_async_copy(hbm[a].at[layer, page], bufs[a].at[sl, p], sem.at[sl, a]))
        return out

    def start(ch, sl):
        for cp in copies(ch, sl):
            cp.start()

    row_f = lax.broadcasted_iota(jnp.int32, (SUBLANES, FOX_W), 0)
    lane_f = lax.broadcasted_iota(jnp.int32, (SUBLANES, FOX_W), 1)
    own_f = (lane_f // HEAD_DIM) == row_f
    row_d = lax.broadcasted_iota(jnp.int32, (SUBLANES, DIFF_W), 0)
    lane_d = lax.broadcasted_iota(jnp.int32, (SUBLANES, DIFF_W), 1)
    own_dq = (lane_d // D_DIFF) == row_d
    own_dv = (lane_d // HEAD_DIM) == (row_d // 2)

    q = q_ref[...]
    qf = jnp.where(own_f, q[:, 0:FOX_W], 0.0).astype(BF16)
    qd = jnp.where(own_dq, q[:, FOX_W:QK_W], 0.0).astype(BF16)
    d_scale = D_DIFF ** -0.5

    def begin():
        kn = kn_ref[...].astype(BF16).astype(F32)
        vn = vn_ref[...].astype(BF16).astype(F32)
        mf_scr[...] = jnp.sum(qf.astype(F32) * kn[:, 0:FOX_W], axis=-1, keepdims=True)
        md_scr[...] = jnp.sum(qd.astype(F32) * kn[:, FOX_W:QK_W], axis=-1, keepdims=True) * d_scale
        lsf_scr[...] = jnp.ones_like(lsf_scr)
        lsd_scr[...] = jnp.ones_like(lsd_scr)
        accf_scr[...] = jnp.broadcast_to(vn[:, 0:FOX_W], accf_scr.shape)
        accd_scr[...] = jnp.broadcast_to(vn[:, FOX_W:QK_W], accd_scr.shape)
        carry_scr[...] = lfn_ref[...]

    def online(slot, s_list, m_scr, l_scr, acc_scr, v_buf):
        m_prev = m_scr[...]
        m_new = m_prev
        for s in s_list:
            m_new = jnp.maximum(m_new, jnp.max(s, axis=-1, keepdims=True))
        alpha = jnp.exp(m_prev - m_new)
        l_new = alpha * l_scr[...]
        acc = alpha * acc_scr[...]
        for p, s in enumerate(s_list):
            e = jnp.exp(s - m_new)
            l_new = l_new + jnp.sum(e, axis=-1, keepdims=True)
            acc = acc + _nt(e.astype(BF16), v_buf[slot, p].astype(BF16))
        m_scr[...] = m_new
        l_scr[...] = l_new
        acc_scr[...] = acc

    def chunk(ch, slot):
        for cp in copies(ch, slot):
            cp.wait()
        utri = utri_ref[...]
        carry = carry_scr[...]
        bias = [None] * npg
        for p in reversed(range(npg)):
            lf = lf_buf[slot, p]
            hi = lf.astype(BF16)
            r1 = lf - hi.astype(F32)
            mid = r1.astype(BF16)
            lo = (r1 - mid.astype(F32)).astype(BF16)
            bias[p] = _nn(hi, utri) + _nn(mid, utri) + _nn(lo, utri) + carry
            carry = carry + jnp.sum(lf, axis=-1, keepdims=True)
        carry_scr[...] = carry
        sf = [_nn(qf, fk_buf[slot, p].astype(BF16)) + bias[p] for p in range(npg)]
        sd = [_nn(qd, dk_buf[slot, p].astype(BF16)) * d_scale for p in range(npg)]
        online(slot, sf, mf_scr, lsf_scr, accf_scr, fv_buf)
        online(slot, sd, md_scr, lsd_scr, accd_scr, dv_buf)

    def finish():
        zf = accf_scr[...] / lsf_scr[...]
        fox = jnp.sum(jnp.where(own_f, zf, 0.0), axis=0, keepdims=True)
        lam = _lambda(lam_ref, lam_init)
        coef = jnp.where((row_d % 2) == 0, 1.0, -lam)
        zd = coef * (accd_scr[...] / lsd_scr[...])
        dif = jnp.sum(jnp.where(own_dv, zd, 0.0), axis=0, keepdims=True)
        sq = dif * dif
        lane1 = lax.broadcasted_iota(jnp.int32, (1, DIFF_W), 1)
        ms = jnp.zeros_like(dif)
        for hd in range(H_DIFF):
            in_head = (lane1 // HEAD_DIM) == hd
            ms_h = jnp.sum(jnp.where(in_head, sq, 0.0), axis=-1, keepdims=True) / HEAD_DIM
            ms = jnp.where(in_head, ms_h, ms)
        dn = dif * lax.rsqrt(ms + LN_EPS) * g_ref[...] * (1.0 - lam_init)
        o_ref[:, 0:FOX_W] = fox.astype(o_ref.dtype)
        o_ref[:, FOX_W:QK_W] = dn.astype(o_ref.dtype)

    return start, begin, chunk, finish


def _decode_kernel(pt_ref, *refs, layer, n_pages, lam_init):
    start, begin, chunk, finish = _decode_ops(pt_ref, *refs, layer=layer, n_pages=n_pages, lam_init=lam_init)
    c = pl.program_id(1)
    n_chunks = pl.num_programs(1)
    ch = pl.program_id(0) * n_chunks + c
    slot = ch % 2

    @pl.when(ch == 0)
    def _():
        start(ch, slot)

    @pl.when(ch + 1 < pl.num_programs(0) * n_chunks)
    def _():
        start(ch + 1, 1 - slot)

    @pl.when(c == 0)
    def _():
        begin()

    chunk(ch, slot)

    @pl.when(c == n_chunks - 1)
    def _():
        finish()


def _decode_attn(page_table, q, kn, vn, lfn, lam4, g256, utri, caches, layer, lam_init):
    nb, n_pages = page_table.shape
    npg = PAGES_PER_STEP
    page = caches[0].shape[-1]
    seq = lambda *blk: pl.BlockSpec((None,) + blk, lambda b, c, pt: (b, 0, 0))
    const = lambda a: pl.BlockSpec(a.shape, lambda b, c, pt: (0,) * a.ndim)
    any_spec = pl.BlockSpec(memory_space=pl.ANY)
    grid_spec = pltpu.PrefetchScalarGridSpec(
        num_scalar_prefetch=1,
        grid=(nb, n_pages // npg),
        in_specs=[seq(1, QK_W), seq(1, QK_W), seq(1, QK_W), seq(SUBLANES, LANES), const(lam4), const(g256),
                  const(utri)] + [any_spec] * 5,
        out_specs=seq(1, QK_W),
        scratch_shapes=[pltpu.VMEM((2, npg, FOX_W, page), F32), pltpu.VMEM((2, npg, FOX_W, page), F32),
                        pltpu.VMEM((2, npg, SUBLANES, page), F32),
                        pltpu.VMEM((2, npg, DIFF_W, page), F32), pltpu.VMEM((2, npg, DIFF_W, page), F32),
                        pltpu.SemaphoreType.DMA((2, 5)),
                        pltpu.VMEM((SUBLANES, 1), F32), pltpu.VMEM((SUBLANES, 1), F32),
                        pltpu.VMEM((SUBLANES, FOX_W), F32),
                        pltpu.VMEM((SUBLANES, 1), F32), pltpu.VMEM((SUBLANES, 1), F32),
                        pltpu.VMEM((SUBLANES, DIFF_W), F32),
                        pltpu.VMEM((SUBLANES, LANES), F32)],
    )
    return pl.pallas_call(
        functools.partial(_decode_kernel, layer=layer, n_pages=n_pages, lam_init=lam_init),
        out_shape=jax.ShapeDtypeStruct((nb, 1, QK_W), BF16),
        grid_spec=grid_spec,
        compiler_params=pltpu.CompilerParams(dimension_semantics=("arbitrary", "arbitrary"),
                                             vmem_limit_bytes=VMEM_LIMIT),
        name="decode_attn",
    )(page_table.reshape(-1), q, kn, vn, lfn, lam4, g256, utri, *caches)


def _pack_w_in(w):
    o = 0
    parts = {}
    for name, width in (("fq", FOX_W), ("fk", FOX_W), ("fv", FOX_W), ("fg", H_FOX), ("dq", DIFF_W),
                        ("dk", DIFF_W), ("dv", DIFF_W)):
        parts[name] = w[:, o:o + width]
        o += width
    conv = w[:, o:]
    fg = jnp.pad(parts["fg"], ((0, 0), (0, LANES - H_FOX)))
    packed = jnp.concatenate([parts["fq"] * (HEAD_DIM ** -0.5), parts["dq"], parts["fk"], parts["dk"],
                              parts["fv"], parts["dv"], conv, fg], axis=1)
    return packed.astype(BF16)


def kernel(x_prompt, x_sample, cache_fox_k, cache_fox_v, cache_fox_logf, cache_diff_k, cache_diff_v, state_conv, page_table, c_prompt, c_sample, w_ada, b_ada, w_in, b_fgate, lambda_q1, lambda_k1, lambda_q2, lambda_k2, diff_norm_g, conv_w, conv_b, conv_ln_g, conv_ln_b, w_o, ln_mix_g, ln_mix_b, w_ffn_in, w_ffn_out, ln_ffn_g, ln_ffn_b):
    depth = w_in.shape[0]
    b, s, d = x_prompt.shape
    nb = x_sample.shape[0]
    conv_ch = d - QK_W
    n_pool, page = cache_fox_k.shape[1:3]
    alpha = (2.0 * depth) ** 0.25

    fkT = jnp.transpose(cache_fox_k, (0, 1, 3, 4, 2)).reshape(depth, n_pool, FOX_W, page)
    fvT = jnp.transpose(cache_fox_v, (0, 1, 3, 4, 2)).reshape(depth, n_pool, FOX_W, page)
    dkT = jnp.transpose(cache_diff_k, (0, 1, 3, 4, 2)).reshape(depth, n_pool, DIFF_W, page)
    dvT = jnp.transpose(cache_diff_v, (0, 1, 3, 4, 2)).reshape(depth, n_pool, DIFF_W, page)
    lfT = jnp.pad(jnp.transpose(cache_fox_logf, (0, 1, 3, 2)), ((0, 0), (0, 0), (0, SUBLANES - H_FOX), (0, 0)))
    conv_hist = jnp.transpose(state_conv, (0, 2, 1, 3))

    ada = _ada(jnp.concatenate([c_prompt, c_sample], axis=0), w_ada.astype(BF16), b_ada)

    tm = min(ROW_TILE, s)
    tri = jnp.tril(jnp.ones((tm, tm), F32)).astype(BF16)
    utri = jnp.triu(jnp.ones((page, page), F32), 1).T.astype(BF16)
    xp, xs = x_prompt, x_sample.reshape(nb, d)
    state = None
    new_p = [[] for _ in range(6)]
    new_s = [[] for _ in range(6)]
    for l in range(depth):
        lam_init = 0.8 - 0.6 * math.exp(-0.3 * l)
        w1 = _pack_w_in(w_in[l])
        bfg = jnp.pad(b_fgate[l], (0, LANES - H_FOX)).reshape(1, LANES)
        lam4 = jnp.stack([lambda_q1[l], lambda_k1[l], lambda_q2[l], lambda_k2[l]])
        g128 = jnp.tile(diff_norm_g[l], 2).reshape(1, LANES)
        g256 = jnp.tile(diff_norm_g[l], H_DIFF).reshape(1, DIFF_W)
        cw, cb = conv_w[l], conv_b[l].reshape(1, conv_ch)
        cg, cbb = conv_ln_g[l].reshape(1, conv_ch), conv_ln_b[l].reshape(1, conv_ch)
        wo = w_o[l].astype(BF16)
        woa, woc = wo[0:QK_W], wo[QK_W:]
        wfi, wfo = w_ffn_in[l].astype(BF16), w_ffn_out[l].astype(BF16)
        g1, b1 = ln_mix_g[l].reshape(1, d), ln_mix_b[l].reshape(1, d)
        g2, b2 = ln_ffn_g[l].reshape(1, d), ln_ffn_b[l].reshape(1, d)
        mods = [ada[l, :, i * d:(i + 1) * d] for i in range(6)]
        mp = [m[0:b].reshape(b, 1, d) for m in mods]
        ms = [m[b:b + nb].reshape(1, nb, d) for m in mods]

        (qT, kr, vT, *state, u, lf_t, cum, cumT) = _inproj_prompt(xp, mp[1], mp[0], w1, bfg, tri, state)
        qs, ks, vs, us, lfs = _inproj_sample(xs, ms[1][0], ms[0][0], w1, bfg)
        ao = _prompt_attn(qT, kr, vT, cum, cumT, lam4, g128, lam_init)
        co = _conv_prompt(u, cw, cb, cg, cbb)
        lfn = jnp.broadcast_to(jnp.pad(lfs[:, 0:H_FOX], ((0, 0), (0, SUBLANES - H_FOX)))[:, :, None],
                               (nb, SUBLANES, LANES))
        r3 = lambda a: a.reshape(nb, 1, a.shape[-1])
        xp = _post(xp, ao, co, (mp[2], mp[3], mp[4], mp[5]), woa, woc, g1, b1, wfi, wfo, g2, b2, alpha)
        aos = _decode_attn(page_table, r3(qs), r3(ks), r3(vs), lfn, lam4, g256, utri,
                           (fkT, fvT, lfT, dkT, dvT), l, lam_init)
        new_p[2].append(jnp.transpose(lf_t[:, 0:H_FOX, :], (0, 2, 1)))
        new_p[5].append(u[:, s - (CONV_WIDTH - 1):, :])

        cos = _conv_sample(conv_hist[l], us, cw, cb, cg, cbb)
        xs = _post(xs.reshape(1, nb, d), aos.reshape(1, nb, QK_W), cos.reshape(1, nb, conv_ch),
                   (ms[2], ms[3], ms[4], ms[5]), woa, woc, g1, b1, wfi, wfo, g2, b2, alpha).reshape(nb, d)
        new_s[0].append(ks[:, 0:FOX_W].reshape(nb, 1, H_FOX, HEAD_DIM))
        new_s[1].append(vs[:, 0:FOX_W].reshape(nb, 1, H_FOX, HEAD_DIM))
        new_s[2].append(lfs[:, 0:H_FOX].reshape(nb, 1, H_FOX))
        new_s[3].append(ks[:, FOX_W:QK_W].reshape(nb, 1, H_DIFF, HEAD_DIM))
        new_s[4].append(vs[:, FOX_W:QK_W].reshape(nb, 1, H_DIFF, HEAD_DIM))
        new_s[5].append(jnp.concatenate([state_conv[l][:, 1:, :], us[:, None, :]], axis=1))

    unT = lambda a, h: jnp.transpose(a.reshape(depth, b, h, HEAD_DIM, s), (0, 1, 4, 2, 3))
    fk_p, fv_p, dk_p, dv_p = state
    return (xp, xs.reshape(nb, 1, d),
            unT(fk_p, H_FOX), unT(fv_p, H_FOX), jnp.stack(new_p[2]),
            unT(dk_p, H_DIFF), unT(dv_p, H_DIFF), jnp.stack(new_p[5]),
            jnp.stack(new_s[0]), jnp.stack(new_s[1]), jnp.stack(new_s[2]),
            jnp.stack(new_s[3]), jnp.stack(new_s[4]), jnp.stack(new_s[5]))
```

```python
import functools
import math

import jax
import jax.numpy as jnp
from jax import lax
from jax.experimental import pallas as pl
from jax.experimental.pallas import tpu as pltpu

F32 = jnp.float32
BF16 = jnp.bfloat16

HEAD_DIM = 64
H_FOX = 6
FOX_W = H_FOX * HEAD_DIM
H_DIFF = 4
D_DIFF = HEAD_DIM // 2
DIFF_W = H_DIFF * HEAD_DIM
QK_W = FOX_W + DIFF_W
CONV_WIDTH = 31
LN_EPS = 1e-5
NEG_INF = -1e30
LANES = 128
SUBLANES = 8
BIAS_LANE = HEAD_DIM
KROW_W = H_FOX * LANES + DIFF_W
VMEM_LIMIT = 56 * 1024 * 1024

ROW_TILE = 512
ATTN_TILE = 512
FFN_CHUNK = 256
PAGES_PER_STEP = 16


def _nn(a, b):
    return jnp.dot(a, b, preferred_element_type=F32)


def _nt(a, b):
    return lax.dot_general(a, b, (((1,), (1,)), ((), ())), preferred_element_type=F32)


def _sigmoid(x):
    return jax.nn.sigmoid(x)


def _log_sigmoid(x):
    return jnp.minimum(x, 0.0) - jnp.log1p(jnp.exp(-jnp.abs(x)))


def _three_bf16(x):
    hi = x.astype(BF16).astype(F32)
    r = x - hi
    mid = r.astype(BF16).astype(F32)
    return hi, mid, (r - mid).astype(BF16).astype(F32)


def _layernorm(x, g, b):
    mu = jnp.mean(x, axis=-1, keepdims=True)
    var = jnp.mean(jnp.square(x - mu), axis=-1, keepdims=True)
    return (x - mu) * lax.rsqrt(var + LN_EPS) * g + b


def _lambda(lam_ref, lam_init):
    a = lam_ref[...]
    s1 = jnp.sum(a[0:1] * a[1:2], axis=-1, keepdims=True)
    s2 = jnp.sum(a[2:3] * a[3:4], axis=-1, keepdims=True)
    return jnp.exp(s1) - jnp.exp(s2) + lam_init


def _const_spec(shape):
    nd = len(shape)
    return pl.BlockSpec(shape, lambda *_: (0,) * nd, pipeline_mode=pl.Buffered(1))


def _ada_kernel(c_ref, w_ref, b_ref, o_ref):
    c = c_ref[...]
    a = (c * _sigmoid(c)).astype(BF16)
    o_ref[...] = _nn(a, w_ref[...].astype(BF16)) + b_ref[...]


def _ada(c_all, w_ada, b_ada):
    depth, d, n = w_ada.shape
    rows = c_all.shape[0]
    tn = d
    return pl.pallas_call(
        _ada_kernel,
        out_shape=jax.ShapeDtypeStruct((depth, rows, n), F32),
        grid=(depth, n // tn),
        in_specs=[pl.BlockSpec((rows, d), lambda l, j: (0, 0)),
                  pl.BlockSpec((None, d, tn), lambda l, j: (l, 0, j)),
                  pl.BlockSpec((None, 1, tn), lambda l, j: (l, 0, j))],
        out_specs=pl.BlockSpec((None, rows, tn), lambda l, j: (l, 0, j)),
        compiler_params=pltpu.CompilerParams(dimension_semantics=("arbitrary", "arbitrary"),
                                             vmem_limit_bytes=VMEM_LIMIT),
        name="ada",
    )(c_all, w_ada, b_ada.reshape(depth, 1, n))


def _inproj_kernel(*refs, transposed, conv_ch, attn_tile, n_prev=0):
    x_ref, sc_ref, sh_ref, w_ref, bfg_ref = refs[:5]
    h = (x_ref[...] * (1.0 + sc_ref[...]) + sh_ref[...]).astype(BF16)
    c0 = 3 * QK_W
    proj = _nn(h, w_ref[...])
    if transposed:
        n_in = 6 + (4 if n_prev else 0)
        tri_ref, prev_refs = refs[5], refs[6:n_in]
        (qT_ref, k_ref, vT_ref, fkT_ref, fvT_ref, dkT_ref, dvT_ref, u_ref, lfT_ref,
         cumT_ref, carry_ref) = refs[n_in:]
        tm = x_ref.shape[0]

        def put_state(o_ref, which, val):
            if n_prev:
                o_ref[0:n_prev] = prev_refs[which][...]
            o_ref[n_prev] = val

        qT_ref[...] = proj[:, 0:QK_W].T.astype(BF16)
        pk = proj[:, QK_W:2 * QK_W]
        k_ref[:, H_FOX * LANES:] = pk[:, FOX_W:QK_W].astype(BF16)
        pkT = pk.T
        put_state(fkT_ref, 0, pkT[0:FOX_W])
        put_state(dkT_ref, 2, pkT[FOX_W:QK_W])
        pvT = proj[:, 2 * QK_W:3 * QK_W].T
        put_state(fvT_ref, 1, pvT[0:FOX_W])
        put_state(dvT_ref, 3, pvT[FOX_W:QK_W])
        pvT_bf = pvT.astype(BF16)
        for t in range(tm // attn_tile):
            vT_ref[t] = pvT_bf[:, t * attn_tile:(t + 1) * attn_tile]
    else:
        q_ref, k_ref, v_ref, u_ref, lf_ref = refs[5:]
        q_ref[...] = proj[:, 0:QK_W]
        k_ref[...] = proj[:, QK_W:2 * QK_W]
        v_ref[...] = proj[:, 2 * QK_W:3 * QK_W]
    u_ref[...] = proj[:, c0:c0 + conv_ch] * _sigmoid(proj[:, c0 + conv_ch:c0 + 2 * conv_ch])
    z = proj[:, c0 + 2 * conv_ch:c0 + 2 * conv_ch + LANES] + bfg_ref[...]
    lane = lax.broadcasted_iota(jnp.int32, z.shape, 1)
    lf = jnp.where(lane < H_FOX, _log_sigmoid(z), 0.0)
    if not transposed:
        lf_ref[...] = lf
        return
    @pl.when(pl.program_id(1) == 0)
    def _():
        carry_ref[...] = jnp.zeros_like(carry_ref)
    hi = lf.astype(BF16)
    r1 = lf - hi.astype(F32)
    mid = r1.astype(BF16)
    lo = (r1 - mid.astype(F32)).astype(BF16)
    tri = tri_ref[...]
    cum = _nn(tri, hi) + _nn(tri, mid) + _nn(tri, lo) + carry_ref[...]
    carry_ref[...] = cum[tm - 1:tm, :]
    cumT_ref[...] = cum.T[0:SUBLANES, :]
    lfT_ref[...] = lf.T[0:SUBLANES, :]
    ones_lanes = ((lane >= BIAS_LANE + 3) & (lane < BIAS_LANE + 6)).astype(F32)
    for hd in range(H_FOX):
        grp = pk[:, (hd // 2) * LANES:(hd // 2 + 1) * LANES]
        if hd % 2:
            grp = pltpu.roll(grp, HEAD_DIM, 1)
        c_hi, c_mid, c_lo = _three_bf16(jnp.broadcast_to(cum[:, hd:hd + 1], grp.shape))
        bias = jnp.where(lane == BIAS_LANE, c_hi,
                         jnp.where(lane == BIAS_LANE + 1, c_mid, jnp.where(lane == BIAS_LANE + 2, c_lo, ones_lanes)))
        k_ref[:, hd * LANES:(hd + 1) * LANES] = jnp.where(lane < HEAD_DIM, grp, bias).astype(BF16)


def _inproj_prompt(x, sc, sh, w1, bfg, tri, prev):
    b, s, d = x.shape
    conv_ch = d - QK_W
    tm = min(ROW_TILE, s)
    ta = min(ATTN_TILE, s)
    nb = s // ta
    n_prev = 0 if prev is None else prev[0].shape[0]
    row = lambda w: pl.BlockSpec((None, tm, w), lambda i, j: (i, j, 0))
    colT = lambda r: pl.BlockSpec((None, r, tm), lambda i, j: (i, 0, j))
    stack = lambda n, r: pl.BlockSpec((n, None, r, tm), lambda i, j: (0, i, 0, j))
    vec = pl.BlockSpec((None, 1, d), lambda i, j: (i, 0, 0))
    state_widths = (FOX_W, FOX_W, DIFF_W, DIFF_W)
    out_shape = (
        jax.ShapeDtypeStruct((b, QK_W, s), BF16),
        jax.ShapeDtypeStruct((b, s, KROW_W), BF16),
        jax.ShapeDtypeStruct((b, nb, QK_W, ta), BF16),
    ) + tuple(jax.ShapeDtypeStruct((n_prev + 1, b, r, s), F32) for r in state_widths) + (
        jax.ShapeDtypeStruct((b, s, conv_ch), F32),
        jax.ShapeDtypeStruct((b, SUBLANES, s), F32),
        jax.ShapeDtypeStruct((b, SUBLANES, s), F32),
    )
    out_specs = (colT(QK_W), row(KROW_W),
                 pl.BlockSpec((None, tm // ta, QK_W, ta), lambda i, j: (i, j, 0, 0)),
                 ) + tuple(stack(n_prev + 1, r) for r in state_widths) + (row(conv_ch),
                 colT(SUBLANES), colT(SUBLANES))
    prev = () if prev is None else tuple(prev)
    return pl.pallas_call(
        functools.partial(_inproj_kernel, transposed=True, conv_ch=conv_ch, attn_tile=ta, n_prev=n_prev),
        out_shape=out_shape,
        grid=(b, s // tm),
        in_specs=[row(d), vec, vec, _const_spec(w1.shape), _const_spec(bfg.shape), _const_spec(tri.shape)]
                 + [stack(n_prev, r) for r, _ in zip(state_widths, prev)],
        out_specs=out_specs,
        scratch_shapes=[pltpu.VMEM((1, LANES), F32)],
        compiler_params=pltpu.CompilerParams(dimension_semantics=("arbitrary", "arbitrary"),
                                             vmem_limit_bytes=VMEM_LIMIT),
        name="inproj_prompt",
    )(x, sc, sh, w1, bfg, tri, *prev)


def _inproj_sample(x, sc, sh, w1, bfg):
    n, d = x.shape
    conv_ch = d - QK_W
    out_shape = (jax.ShapeDtypeStruct((n, QK_W), F32), jax.ShapeDtypeStruct((n, QK_W), F32),
                 jax.ShapeDtypeStruct((n, QK_W), F32), jax.ShapeDtypeStruct((n, conv_ch), F32),
                 jax.ShapeDtypeStruct((n, LANES), F32))
    return pl.pallas_call(
        functools.partial(_inproj_kernel, transposed=False, conv_ch=conv_ch, attn_tile=0),
        out_shape=out_shape,
        compiler_params=pltpu.CompilerParams(vmem_limit_bytes=VMEM_LIMIT),
        name="inproj_sample",
    )(x, sc, sh, w1, bfg)


N_GROUPS = QK_W // LANES
N_FOX_GROUPS = FOX_W // LANES
MAPS_PER_FOX_GROUP = LANES // HEAD_DIM
MAPS_PER_DIFF_GROUP = LANES // D_DIFF
N_MAPS = N_FOX_GROUPS * MAPS_PER_FOX_GROUP + (N_GROUPS - N_FOX_GROUPS) * MAPS_PER_DIFF_GROUP
LOG2E = math.log2(math.e)
SCORE_LOOKAHEAD = 4


def _group_maps(g):
    if g < N_FOX_GROUPS:
        return g * MAPS_PER_FOX_GROUP, MAPS_PER_FOX_GROUP, HEAD_DIM
    first = N_FOX_GROUPS * MAPS_PER_FOX_GROUP + (g - N_FOX_GROUPS) * MAPS_PER_DIFF_GROUP
    return first, MAPS_PER_DIFF_GROUP, D_DIFF


def _prompt_attn_kernel(qT_ref, k_ref, vT_ref, cumT_ref, lam_ref, g_ref, o_ref,
                        qm_scr, m_scr, l_scr, acc_scr, *, lam_init):
    i = pl.program_id(1)
    t = qT_ref.shape[1]
    row128 = lax.broadcasted_iota(jnp.int32, (LANES, t), 0)
    row64 = lax.broadcasted_iota(jnp.int32, (HEAD_DIM, t), 0)

    for hd in range(H_FOX):
        r0 = (hd // 2) * LANES + (hd % 2) * HEAD_DIM
        c_hi, c_mid, c_lo = _three_bf16(cumT_ref[hd:hd + 1, :])
        bias_rows = jnp.where(row64 < 3, -1.0,
                              jnp.where(row64 == 3, c_hi, jnp.where(row64 == 4, c_mid,
                                                                    jnp.where(row64 == 5, c_lo, 0.0))))
        qm_scr[hd, 0:HEAD_DIM, :] = qT_ref[r0:r0 + HEAD_DIM, :]
        qm_scr[hd, HEAD_DIM:LANES, :] = bias_rows.astype(BF16)
    for g in range(N_FOX_GROUPS, N_GROUPS):
        first, n, width = _group_maps(g)
        qT = qT_ref[g * LANES:(g + 1) * LANES, :].astype(F32)
        for mp in range(n):
            qm_scr[first + mp] = jnp.where((row128 // width) == mp, qT, 0.0).astype(BF16)
    m_scr[...] = jnp.full_like(m_scr, NEG_INF)
    l_scr[...] = jnp.zeros_like(l_scr)
    acc_scr[...] = jnp.zeros_like(acc_scr)

    map_group = [g for g in range(N_GROUPS) for _ in range(_group_maps(g)[1])]

    def step(j, masked):
        start = pl.multiple_of(j * t, t)
        k_tiles, vT_tiles = {}, {}
        if masked:
            causal = lax.broadcasted_iota(jnp.int32, (t, t), 0) <= lax.broadcasted_iota(jnp.int32, (t, t), 1)

        def score(mi):
            kg = mi if mi < H_FOX else H_FOX + map_group[mi] - N_FOX_GROUPS
            if kg not in k_tiles:
                k_tiles[kg] = k_ref[pl.ds(start, t), kg * LANES:(kg + 1) * LANES]
            return _nn(k_tiles[kg], qm_scr[mi])

        def update(mi, s):
            g = map_group[mi]
            if g not in vT_tiles:
                vT_tiles[g] = vT_ref[j, g * LANES:(g + 1) * LANES, :]
            c2 = LOG2E if g < N_FOX_GROUPS else LOG2E * D_DIFF ** -0.5
            if masked:
                s = jnp.where(causal, s, NEG_INF)
            m_prev = m_scr[mi]
            m_new = jnp.maximum(m_prev, jnp.max(s, axis=0, keepdims=True))
            alpha = jnp.exp2((m_prev - m_new) * c2)
            p = jnp.exp2((s - m_new) * c2)
            l_scr[mi] = alpha * l_scr[mi] + jnp.sum(p, axis=0, keepdims=True)
            acc_scr[mi] = alpha * acc_scr[mi] + _nn(vT_tiles[g], p.astype(BF16))
            m_scr[mi] = m_new

        pending = []
        for idx in range(N_MAPS + SCORE_LOOKAHEAD):
            if idx < N_MAPS:
                pending.append((idx, score(idx)))
            if idx >= SCORE_LOOKAHEAD:
                update(*pending.pop(0))

    def body(j, carry):
        step(j, False)
        return carry

    lax.fori_loop(0, i, body, 0)
    step(i, True)

    def out_T(mi):
        return acc_scr[mi] / l_scr[mi]

    lam = _lambda(lam_ref, lam_init)
    low = row128 < HEAD_DIM
    for g in range(N_GROUPS):
        first, n, _ = _group_maps(g)
        cols = slice(g * LANES, (g + 1) * LANES)
        if g < N_FOX_GROUPS:
            o_ref[:, cols] = jnp.where(low, out_T(first), out_T(first + 1)).T.astype(o_ref.dtype)
        else:
            dT = jnp.where(low, out_T(first) - lam * out_T(first + 1), out_T(first + 2) - lam * out_T(first + 3))
            sq = dT * dT
            ms0 = jnp.mean(sq[0:HEAD_DIM], axis=0, keepdims=True)
            ms1 = jnp.mean(sq[HEAD_DIM:LANES], axis=0, keepdims=True)
            ms = jnp.where(low, ms0, ms1)
            dn = (dT * lax.rsqrt(ms + LN_EPS)).T * g_ref[...] * (1.0 - lam_init)
            o_ref[:, cols] = dn.astype(o_ref.dtype)


def _prompt_attn(qT, k, vT, cumT, lam4, g128, lam_init):
    b, s, _ = k.shape
    t = vT.shape[-1]
    return pl.pallas_call(
        functools.partial(_prompt_attn_kernel, lam_init=lam_init),
        out_shape=jax.ShapeDtypeStruct((b, s, QK_W), BF16),
        grid=(b, s // t),
        in_specs=[pl.BlockSpec((None, QK_W, t), lambda bi, i: (bi, 0, i)),
                  pl.BlockSpec((None, s, KROW_W), lambda bi, i: (bi, 0, 0)),
                  pl.BlockSpec((None, s // t, QK_W, t), lambda bi, i: (bi, 0, 0, 0)),
                  pl.BlockSpec((None, SUBLANES, t), lambda bi, i: (bi, 0, i)),
                  _const_spec(lam4.shape), _const_spec(g128.shape)],
        out_specs=pl.BlockSpec((None, t, QK_W), lambda bi, i: (bi, i, 0)),
        scratch_shapes=[pltpu.VMEM((N_MAPS, LANES, t), BF16),
                        pltpu.VMEM((N_MAPS, 1, t), F32), pltpu.VMEM((N_MAPS, 1, t), F32),
                        pltpu.VMEM((N_MAPS, LANES, t), F32)],
        compiler_params=pltpu.CompilerParams(dimension_semantics=("arbitrary", "arbitrary"),
                                             vmem_limit_bytes=VMEM_LIMIT),
        name="prompt_attn",
    )(qT, k, vT, cumT, lam4, g128)


HALO = 32


def _conv_finish(y, cb_ref, g_ref, b_ref):
    y = _layernorm(y + cb_ref[...], g_ref[...], b_ref[...])
    return y * _sigmoid(y)


def _conv_prompt_kernel(u_ref, halo_ref, cw_ref, cb_ref, g_ref, b_ref, o_ref, win_scr, shift_scr):
    tm = u_ref.shape[0]
    j = pl.program_id(1)
    halo = halo_ref[...]
    win_scr[0:HALO, :] = jnp.where(j == 0, jnp.zeros_like(halo), halo)
    win_scr[HALO:HALO + tm, :] = u_ref[...]
    base = HALO - (CONV_WIDTH - 1)
    acc = None
    for phase in range(SUBLANES):
        taps = [w for w in range(CONV_WIDTH) if (base + w) % SUBLANES == phase]
        span = base + taps[-1] - phase + tm
        shift_scr[0:span, :] = win_scr[phase:phase + span, :]
        for w in taps:
            o = base + w - phase
            term = shift_scr[o:o + tm, :] * cw_ref[w:w + 1, :]
            acc = term if acc is None else acc + term
    o_ref[...] = _conv_finish(acc, cb_ref, g_ref, b_ref).astype(o_ref.dtype)


def _conv_prompt(u, cw, cb, g, bb):
    b, s, c = u.shape
    tm = min(ROW_TILE, s)
    hb = tm // HALO
    return pl.pallas_call(
        _conv_prompt_kernel,
        out_shape=jax.ShapeDtypeStruct((b, s, c), BF16),
        grid=(b, s // tm),
        in_specs=[pl.BlockSpec((None, tm, c), lambda i, j: (i, j, 0)),
                  pl.BlockSpec((None, HALO, c), lambda i, j: (i, jnp.maximum(j * hb - 1, 0), 0)),
                  _const_spec(cw.shape), _const_spec(cb.shape), _const_spec(g.shape), _const_spec(bb.shape)],
        out_specs=pl.BlockSpec((None, tm, c), lambda i, j: (i, j, 0)),
        scratch_shapes=[pltpu.VMEM((HALO + tm, c), F32), pltpu.VMEM((HALO + tm, c), F32)],
        compiler_params=pltpu.CompilerParams(dimension_semantics=("arbitrary", "arbitrary"),
                                             vmem_limit_bytes=VMEM_LIMIT),
        name="conv_prompt",
    )(u, u, cw, cb, g, bb)


def _conv_sample_kernel(buf_ref, u_ref, cw_ref, cb_ref, g_ref, b_ref, o_ref):
    acc = u_ref[...] * cw_ref[CONV_WIDTH - 1:CONV_WIDTH, :]
    for w in range(CONV_WIDTH - 1):
        acc = acc + buf_ref[w] * cw_ref[w:w + 1, :]
    o_ref[...] = _conv_finish(acc, cb_ref, g_ref, b_ref).astype(o_ref.dtype)


def _conv_sample(buf_t, u, cw, cb, g, bb):
    return pl.pallas_call(
        _conv_sample_kernel,
        out_shape=jax.ShapeDtypeStruct(u.shape, BF16),
        name="conv_sample",
    )(buf_t, u, cw, cb, g, bb)


def _post_kernel(x_ref, ao_ref, co_ref, ga_ref, shf_ref, scf_ref, gf_ref, woa_ref, woc_ref,
                 g1_ref, b1_ref, win_ref, wout_ref, g2_ref, b2_ref, o_ref, a_scr, *, alpha, d_ff):
    y = _nn(ao_ref[...], woa_ref[...]) + _nn(co_ref[...], woc_ref[...])
    x1 = _layernorm(alpha * x_ref[...] + (1.0 + ga_ref[...]) * y, g1_ref[...], b1_ref[...])
    h = (x1 * (1.0 + scf_ref[...]) + shf_ref[...]).astype(BF16)
    for c in range(d_ff // FFN_CHUNK):
        lo = c * FFN_CHUNK
        gate = _nn(h, win_ref[:, lo:lo + FFN_CHUNK])
        up = _nn(h, win_ref[:, d_ff + lo:d_ff + lo + FFN_CHUNK])
        a_scr[:, lo:lo + FFN_CHUNK] = (gate * _sigmoid(gate) * up).astype(BF16)
    y2 = _nn(a_scr[...], wout_ref[...])
    o_ref[...] = _layernorm(alpha * x1 + (1.0 + gf_ref[...]) * y2, g2_ref[...], b2_ref[...])


def _post(x, ao, co, mods, woa, woc, g1, b1, w_in, w_out, g2, b2, alpha):
    gdim, r, d = x.shape
    d_ff = w_out.shape[0]
    tm = min(ROW_TILE, r)
    row = lambda w: pl.BlockSpec((None, tm, w), lambda i, j: (i, j, 0))

    def mod_spec(m):
        if m.shape[1] == 1:
            return pl.BlockSpec((None, 1, d), lambda i, j: (i, 0, 0))
        return row(d)

    consts = (woa, woc, g1, b1, w_in, w_out, g2, b2)
    return pl.pallas_call(
        functools.partial(_post_kernel, alpha=alpha, d_ff=d_ff),
        out_shape=jax.ShapeDtypeStruct(x.shape, F32),
        grid=(gdim, r // tm),
        in_specs=[row(d), row(ao.shape[-1]), row(co.shape[-1])] + [mod_spec(m) for m in mods]
                 + [_const_spec(c.shape) for c in consts],
        out_specs=row(d),
        scratch_shapes=[pltpu.VMEM((tm, d_ff), BF16)],
        compiler_params=pltpu.CompilerParams(dimension_semantics=("arbitrary", "arbitrary"),
                                             vmem_limit_bytes=VMEM_LIMIT),
        name="post",
    )(x, ao, co, *mods, *consts)


def _decode_ops(pt_ref, q_ref, kn_ref, vn_ref, lfn_ref, lam_ref, g_ref, utri_ref,
                fk_hbm, fv_hbm, lf_hbm, dk_hbm, dv_hbm, o_ref,
                fk_buf, fv_buf, lf_buf, dk_buf, dv_buf, sem,
                mf_scr, lsf_scr, accf_scr, md_scr, lsd_scr, accd_scr, carry_scr,
                *, layer, n_pages, lam_init):
    npg = PAGES_PER_STEP
    n_chunks = n_pages // npg
    hbm = (fk_hbm, fv_hbm, lf_hbm, dk_hbm, dv_hbm)
    bufs = (fk_buf, fv_buf, lf_buf, dk_buf, dv_buf)

    def copies(ch, sl):
        bb = ch // n_chunks
        cc = ch % n_chunks
        first = bb * n_pages + (n_chunks - 1 - cc) * npg
        out = []
        for p in range(npg):
            page = pt_ref[first + p]
            for a in range(5):
                out.append(pltpu.make_async_copy(hbm[a].at[layer, page], bufs[a].at[sl, p], sem.at[sl, a]))
        return out

    def start(ch, sl):
        for cp in copies(ch, sl):
            cp.start()

    row_f = lax.broadcasted_iota(jnp.int32, (SUBLANES, FOX_W), 0)
    lane_f = lax.broadcasted_iota(jnp.int32, (SUBLANES, FOX_W), 1)
    own_f = (lane_f // HEAD_DIM) == row_f
    row_d = lax.broadcasted_iota(jnp.int32, (SUBLANES, DIFF_W), 0)
    lane_d = lax.broadcasted_iota(jnp.int32, (SUBLANES, DIFF_W), 1)
    own_dq = (lane_d // D_DIFF) == row_d
    own_dv = (lane_d // HEAD_DIM) == (row_d // 2)

    q = q_ref[...]
    qf = jnp.where(own_f, q[:, 0:FOX_W], 0.0).astype(BF16)
    qd = jnp.where(own_dq, q[:, FOX_W:QK_W], 0.0).astype(BF16)
    d_scale = D_DIFF ** -0.5

    def begin():
        kn = kn_ref[...].astype(BF16).astype(F32)
        vn = vn_ref[...].astype(BF16).astype(F32)
        mf_scr[...] = jnp.sum(qf.astype(F32) * kn[:, 0:FOX_W], axis=-1, keepdims=True)
        md_scr[...] = jnp.sum(qd.astype(F32) * kn[:, FOX_W:QK_W], axis=-1, keepdims=True) * d_scale
        lsf_scr[...] = jnp.ones_like(lsf_scr)
        lsd_scr[...] = jnp.ones_like(lsd_scr)
        accf_scr[...] = jnp.broadcast_to(vn[:, 0:FOX_W], accf_scr.shape)
        accd_scr[...] = jnp.broadcast_to(vn[:, FOX_W:QK_W], accd_scr.shape)
        carry_scr[...] = lfn_ref[...]

    def online(slot, s_list, m_scr, l_scr, acc_scr, v_buf):
        m_prev = m_scr[...]
        m_new = m_prev
        for s in s_list:
            m_new = jnp.maximum(m_new, jnp.max(s, axis=-1, keepdims=True))
        alpha = jnp.exp(m_prev - m_new)
        l_new = alpha * l_scr[...]
        acc = alpha * acc_scr[...]
        for p, s in enumerate(s_list):
            e = jnp.exp(s - m_new)
            l_new = l_new + jnp.sum(e, axis=-1, keepdims=True)
            acc = acc + _nt(e.astype(BF16), v_buf[slot, p].astype(BF16))
        m_scr[...] = m_new
        l_scr[...] = l_new
        acc_scr[...] = acc

    def chunk(ch, slot):
        for cp in copies(ch, slot):
            cp.wait()
        utri = utri_ref[...]
        carry = carry_scr[...]
        bias = [None] * npg
        for p in reversed(range(npg)):
            lf = lf_buf[slot, p]
            hi = lf.astype(BF16)
            r1 = lf - hi.astype(F32)
            mid = r1.astype(BF16)
            lo = (r1 - mid.astype(F32)).astype(BF16)
            bias[p] = _nn(hi, utri) + _nn(mid, utri) + _nn(lo, utri) + carry
            carry = carry + jnp.sum(lf, axis=-1, keepdims=True)
        carry_scr[...] = carry
        sf = [_nn(qf, fk_buf[slot, p].astype(BF16)) + bias[p] for p in range(npg)]
        sd = [_nn(qd, dk_buf[slot, p].astype(BF16)) * d_scale for p in range(npg)]
        online(slot, sf, mf_scr, lsf_scr, accf_scr, fv_buf)
        online(slot, sd, md_scr, lsd_scr, accd_scr, dv_buf)

    def finish():
        zf = accf_scr[...] / lsf_scr[...]
        fox = jnp.sum(jnp.where(own_f, zf, 0.0), axis=0, keepdims=True)
        lam = _lambda(lam_ref, lam_init)
        coef = jnp.where((row_d % 2) == 0, 1.0, -lam)
        zd = coef * (accd_scr[...] / lsd_scr[...])
        dif = jnp.sum(jnp.where(own_dv, zd, 0.0), axis=0, keepdims=True)
        sq = dif * dif
        lane1 = lax.broadcasted_iota(jnp.int32, (1, DIFF_W), 1)
        ms = jnp.zeros_like(dif)
        for hd in range(H_DIFF):
            in_head = (lane1 // HEAD_DIM) == hd
            ms_h = jnp.sum(jnp.where(in_head, sq, 0.0), axis=-1, keepdims=True) / HEAD_DIM
            ms = jnp.where(in_head, ms_h, ms)
        dn = dif * lax.rsqrt(ms + LN_EPS) * g_ref[...] * (1.0 - lam_init)
        o_ref[:, 0:FOX_W] = fox.astype(o_ref.dtype)
        o_ref[:, FOX_W:QK_W] = dn.astype(o_ref.dtype)

    return start, begin, chunk, finish


def _decode_kernel(pt_ref, *refs, layer, n_pages, lam_init):
    start, begin, chunk, finish = _decode_ops(pt_ref, *refs, layer=layer, n_pages=n_pages, lam_init=lam_init)
    c = pl.program_id(1)
    n_chunks = pl.num_programs(1)
    ch = pl.program_id(0) * n_chunks + c
    slot = ch % 2

    @pl.when(ch == 0)
    def _():
        start(ch, slot)

    @pl.when(ch + 1 < pl.num_programs(0) * n_chunks)
    def _():
        start(ch + 1, 1 - slot)

    @pl.when(c == 0)
    def _():
        begin()

    chunk(ch, slot)

    @pl.when(c == n_chunks - 1)
    def _():
        finish()


def _decode_attn(page_table, q, kn, vn, lfn, lam4, g256, utri, caches, layer, lam_init):
    nb, n_pages = page_table.shape
    npg = PAGES_PER_STEP
    page = caches[0].shape[-1]
    seq = lambda *blk: pl.BlockSpec((None,) + blk, lambda b, c, pt: (b, 0, 0))
    const = lambda a: pl.BlockSpec(a.shape, lambda b, c, pt: (0,) * a.ndim)
    any_spec = pl.BlockSpec(memory_space=pl.ANY)
    grid_spec = pltpu.PrefetchScalarGridSpec(
        num_scalar_prefetch=1,
        grid=(nb, n_pages // npg),
        in_specs=[seq(1, QK_W), seq(1, QK_W), seq(1, QK_W), seq(SUBLANES, LANES), const(lam4), const(g256),
                  const(utri)] + [any_spec] * 5,
        out_specs=seq(1, QK_W),
        scratch_shapes=[pltpu.VMEM((2, npg, FOX_W, page), F32), pltpu.VMEM((2, npg, FOX_W, page), F32),
                        pltpu.VMEM((2, npg, SUBLANES, page), F32),
                        pltpu.VMEM((2, npg, DIFF_W, page), F32), pltpu.VMEM((2, npg, DIFF_W, page), F32),
                        pltpu.SemaphoreType.DMA((2, 5)),
                        pltpu.VMEM((SUBLANES, 1), F32), pltpu.VMEM((SUBLANES, 1), F32),
                        pltpu.VMEM((SUBLANES, FOX_W), F32),
                        pltpu.VMEM((SUBLANES, 1), F32), pltpu.VMEM((SUBLANES, 1), F32),
                        pltpu.VMEM((SUBLANES, DIFF_W), F32),
                        pltpu.VMEM((SUBLANES, LANES), F32)],
    )
    return pl.pallas_call(
        functools.partial(_decode_kernel, layer=layer, n_pages=n_pages, lam_init=lam_init),
        out_shape=jax.ShapeDtypeStruct((nb, 1, QK_W), BF16),
        grid_spec=grid_spec,
        compiler_params=pltpu.CompilerParams(dimension_semantics=("arbitrary", "arbitrary"),
                                             vmem_limit_bytes=VMEM_LIMIT),
        name="decode_attn",
    )(page_table.reshape(-1), q, kn, vn, lfn, lam4, g256, utri, *caches)


def _pack_w_in(w):
    o = 0
    parts = {}
    for name, width in (("fq", FOX_W), ("fk", FOX_W), ("fv", FOX_W), ("fg", H_FOX), ("dq", DIFF_W),
                        ("dk", DIFF_W), ("dv", DIFF_W)):
        parts[name] = w[:, o:o + width]
        o += width
    conv = w[:, o:]
    fg = jnp.pad(parts["fg"], ((0, 0), (0, LANES - H_FOX)))
    packed = jnp.concatenate([parts["fq"] * (HEAD_DIM ** -0.5), parts["dq"], parts["fk"], parts["dk"],
                              parts["fv"], parts["dv"], conv, fg], axis=1)
    return packed.astype(BF16)


def kernel(x_prompt, x_sample, cache_fox_k, cache_fox_v, cache_fox_logf, cache_diff_k, cache_diff_v, state_conv, page_table, c_prompt, c_sample, w_ada, b_ada, w_in, b_fgate, lambda_q1, lambda_k1, lambda_q2, lambda_k2, diff_norm_g, conv_w, conv_b, conv_ln_g, conv_ln_b, w_o, ln_mix_g, ln_mix_b, w_ffn_in, w_ffn_out, ln_ffn_g, ln_ffn_b):
    depth = w_in.shape[0]
    b, s, d = x_prompt.shape
    nb = x_sample.shape[0]
    conv_ch = d - QK_W
    n_pool, page = cache_fox_k.shape[1:3]
    alpha = (2.0 * depth) ** 0.25

    fkT = jnp.transpose(cache_fox_k, (0, 1, 3, 4, 2)).reshape(depth, n_pool, FOX_W, page)
    fvT = jnp.transpose(cache_fox_v, (0, 1, 3, 4, 2)).reshape(depth, n_pool, FOX_W, page)
    dkT = jnp.transpose(cache_diff_k, (0, 1, 3, 4, 2)).reshape(depth, n_pool, DIFF_W, page)
    dvT = jnp.transpose(cache_diff_v, (0, 1, 3, 4, 2)).reshape(depth, n_pool, DIFF_W, page)
    lfT = jnp.pad(jnp.transpose(cache_fox_logf, (0, 1, 3, 2)), ((0, 0), (0, 0), (0, SUBLANES - H_FOX), (0, 0)))
    conv_hist = jnp.transpose(state_conv, (0, 2, 1, 3))

    ada = _ada(jnp.concatenate([c_prompt, c_sample], axis=0), w_ada, b_ada)

    tm = min(ROW_TILE, s)
    tri = jnp.tril(jnp.ones((tm, tm), F32)).astype(BF16)
    utri = jnp.triu(jnp.ones((page, page), F32), 1).T.astype(BF16)
    xp, xs = x_prompt, x_sample.reshape(nb, d)
    state = None
    new_p = [[] for _ in range(6)]
    new_s = [[] for _ in range(6)]
    for l in range(depth):
        lam_init = 0.8 - 0.6 * math.exp(-0.3 * l)
        w1 = _pack_w_in(w_in[l])
        bfg = jnp.pad(b_fgate[l], (0, LANES - H_FOX)).reshape(1, LANES)
        lam4 = jnp.stack([lambda_q1[l], lambda_k1[l], lambda_q2[l], lambda_k2[l]])
        g128 = jnp.tile(diff_norm_g[l], 2).reshape(1, LANES)
        g256 = jnp.tile(diff_norm_g[l], H_DIFF).reshape(1, DIFF_W)
        cw, cb = conv_w[l], conv_b[l].reshape(1, conv_ch)
        cg, cbb = conv_ln_g[l].reshape(1, conv_ch), conv_ln_b[l].reshape(1, conv_ch)
        wo = w_o[l].astype(BF16)
        woa, woc = wo[0:QK_W], wo[QK_W:]
        wfi, wfo = w_ffn_in[l].astype(BF16), w_ffn_out[l].astype(BF16)
        g1, b1 = ln_mix_g[l].reshape(1, d), ln_mix_b[l].reshape(1, d)
        g2, b2 = ln_ffn_g[l].reshape(1, d), ln_ffn_b[l].reshape(1, d)
        mods = [ada[l, :, i * d:(i + 1) * d] for i in range(6)]
        mp = [m[0:b].reshape(b, 1, d) for m in mods]
        ms = [m[b:b + nb].reshape(1, nb, d) for m in mods]

        (qT, kr, vT, *state, u, lf_t, cumT) = _inproj_prompt(xp, mp[1], mp[0], w1, bfg, tri, state)
        qs, ks, vs, us, lfs = _inproj_sample(xs, ms[1][0], ms[0][0], w1, bfg)
        ao = _prompt_attn(qT, kr, vT, cumT, lam4, g128, lam_init)
        co = _conv_prompt(u, cw, cb, cg, cbb)
        lfn = jnp.broadcast_to(jnp.pad(lfs[:, 0:H_FOX], ((0, 0), (0, SUBLANES - H_FOX)))[:, :, None],
                               (nb, SUBLANES, LANES))
        r3 = lambda a: a.reshape(nb, 1, a.shape[-1])
        xp = _post(xp, ao, co, (mp[2], mp[3], mp[4], mp[5]), woa, woc, g1, b1, wfi, wfo, g2, b2, alpha)
        aos = _decode_attn(page_table, r3(qs), r3(ks), r3(vs), lfn, lam4, g256, utri,
                           (fkT, fvT, lfT, dkT, dvT), l, lam_init)
        new_p[2].append(jnp.transpose(lf_t[:, 0:H_FOX, :], (0, 2, 1)))
        new_p[5].append(u[:, s - (CONV_WIDTH - 1):, :])

        cos = _conv_sample(conv_hist[l], us, cw, cb, cg, cbb)
        xs = _post(xs.reshape(1, nb, d), aos.reshape(1, nb, QK_W), cos.reshape(1, nb, conv_ch),
                   (ms[2], ms[3], ms[4], ms[5]), woa, woc, g1, b1, wfi, wfo, g2, b2, alpha).reshape(nb, d)
        new_s[0].append(ks[:, 0:FOX_W].reshape(nb, 1, H_FOX, HEAD_DIM))
        new_s[1].append(vs[:, 0:FOX_W].reshape(nb, 1, H_FOX, HEAD_DIM))
        new_s[2].append(lfs[:, 0:H_FOX].reshape(nb, 1, H_FOX))
        new_s[3].append(ks[:, FOX_W:QK_W].reshape(nb, 1, H_DIFF, HEAD_DIM))
        new_s[4].append(vs[:, FOX_W:QK_W].reshape(nb, 1, H_DIFF, HEAD_DIM))
        new_s[5].append(jnp.concatenate([state_conv[l][:, 1:, :], us[:, None, :]], axis=1))

    unT = lambda a, h: jnp.transpose(a.reshape(depth, b, h, HEAD_DIM, s), (0, 1, 4, 2, 3))
    fk_p, fv_p, dk_p, dv_p = state
    return (xp, xs.reshape(nb, 1, d),
            unT(fk_p, H_FOX), unT(fv_p, H_FOX), jnp.stack(new_p[2]),
            unT(dk_p, H_DIFF), unT(dv_p, H_DIFF), jnp.stack(new_p[5]),
            jnp.stack(new_s[0]), jnp.stack(new_s[1]), jnp.stack(new_s[2]),
            jnp.stack(new_s[3]), jnp.stack(new_s[4]), jnp.stack(new_s[5]))
```

```python
import functools
import math

import jax
import jax.numpy as jnp
from jax import lax
from jax.experimental import pallas as pl
from jax.experimental.pallas import tpu as pltpu

F32 = jnp.float32
BF16 = jnp.bfloat16

HEAD_DIM = 64
H_FOX = 6
FOX_W = H_FOX * HEAD_DIM
H_DIFF = 4
D_DIFF = HEAD_DIM // 2
DIFF_W = H_DIFF * HEAD_DIM
QK_W = FOX_W + DIFF_W
CONV_WIDTH = 31
LN_EPS = 1e-5
NEG_INF = -1e30
LANES = 128
SUBLANES = 8
BIAS_LANE = HEAD_DIM
KROW_W = H_FOX * LANES + DIFF_W
VMEM_LIMIT = 56 * 1024 * 1024

ROW_TILE = 512
ATTN_TILE = 512
FFN_CHUNK = 256
PAGES_PER_STEP = 16


def _nn(a, b):
    return jnp.dot(a, b, preferred_element_type=F32)


def _nt(a, b):
    return lax.dot_general(a, b, (((1,), (1,)), ((), ())), preferred_element_type=F32)


def _sigmoid(x):
    return jax.nn.sigmoid(x)


def _log_sigmoid(x):
    return jnp.minimum(x, 0.0) - jnp.log1p(jnp.exp(-jnp.abs(x)))


def _three_bf16(x):
    hi = x.astype(BF16).astype(F32)
    r = x - hi
    mid = r.astype(BF16).astype(F32)
    return hi, mid, (r - mid).astype(BF16).astype(F32)


def _layernorm(x, g, b):
    mu = jnp.mean(x, axis=-1, keepdims=True)
    var = jnp.mean(jnp.square(x - mu), axis=-1, keepdims=True)
    return (x - mu) * lax.rsqrt(var + LN_EPS) * g + b


def _lambda(lam_ref, lam_init):
    a = lam_ref[...]
    s1 = jnp.sum(a[0:1] * a[1:2], axis=-1, keepdims=True)
    s2 = jnp.sum(a[2:3] * a[3:4], axis=-1, keepdims=True)
    return jnp.exp(s1) - jnp.exp(s2) + lam_init


def _const_spec(shape):
    nd = len(shape)
    return pl.BlockSpec(shape, lambda *_: (0,) * nd, pipeline_mode=pl.Buffered(1))


def _ada_kernel(c_ref, w_ref, b_ref, o_ref):
    c = c_ref[...]
    a = (c * _sigmoid(c)).astype(BF16)
    o_ref[...] = _nn(a, w_ref[...].astype(BF16)) + b_ref[...]


def _ada(c_all, w_ada, b_ada):
    depth, d, n = w_ada.shape
    rows = c_all.shape[0]
    tn = d
    return pl.pallas_call(
        _ada_kernel,
        out_shape=jax.ShapeDtypeStruct((depth, rows, n), F32),
        grid=(depth, n // tn),
        in_specs=[pl.BlockSpec((rows, d), lambda l, j: (0, 0)),
                  pl.BlockSpec((None, d, tn), lambda l, j: (l, 0, j)),
                  pl.BlockSpec((None, 1, tn), lambda l, j: (l, 0, j))],
        out_specs=pl.BlockSpec((None, rows, tn), lambda l, j: (l, 0, j)),
        compiler_params=pltpu.CompilerParams(dimension_semantics=("arbitrary", "arbitrary"),
                                             vmem_limit_bytes=VMEM_LIMIT),
        name="ada",
    )(c_all, w_ada, b_ada.reshape(depth, 1, n))


def _inproj_kernel(*refs, transposed, conv_ch, attn_tile, n_prev=0):
    x_ref, sc_ref, sh_ref, w_ref, bfg_ref = refs[:5]
    h = (x_ref[...] * (1.0 + sc_ref[...]) + sh_ref[...]).astype(BF16)
    c0 = 3 * QK_W
    proj = _nn(h, w_ref[...])
    if transposed:
        n_in = 6 + (4 if n_prev else 0)
        tri_ref, prev_refs = refs[5], refs[6:n_in]
        (qT_ref, k_ref, vT_ref, fkT_ref, fvT_ref, dkT_ref, dvT_ref, u_ref, lfT_ref,
         cumT_ref, carry_ref) = refs[n_in:]
        tm = x_ref.shape[0]

        def put_state(o_ref, which, val):
            if n_prev:
                o_ref[0:n_prev] = prev_refs[which][...]
            o_ref[n_prev] = val

        qT_ref[...] = proj[:, 0:QK_W].T.astype(BF16)
        pk = proj[:, QK_W:2 * QK_W]
        k_ref[:, H_FOX * LANES:] = pk[:, FOX_W:QK_W].astype(BF16)
        pkT = pk.T
        put_state(fkT_ref, 0, pkT[0:FOX_W])
        put_state(dkT_ref, 2, pkT[FOX_W:QK_W])
        pvT = proj[:, 2 * QK_W:3 * QK_W].T
        put_state(fvT_ref, 1, pvT[0:FOX_W])
        put_state(dvT_ref, 3, pvT[FOX_W:QK_W])
        pvT_bf = pvT.astype(BF16)
        for t in range(tm // attn_tile):
            vT_ref[t] = pvT_bf[:, t * attn_tile:(t + 1) * attn_tile]
    else:
        q_ref, k_ref, v_ref, u_ref, lf_ref = refs[5:]
        q_ref[...] = proj[:, 0:QK_W]
        k_ref[...] = proj[:, QK_W:2 * QK_W]
        v_ref[...] = proj[:, 2 * QK_W:3 * QK_W]
    u_ref[...] = proj[:, c0:c0 + conv_ch] * _sigmoid(proj[:, c0 + conv_ch:c0 + 2 * conv_ch])
    z = proj[:, c0 + 2 * conv_ch:c0 + 2 * conv_ch + LANES] + bfg_ref[...]
    lane = lax.broadcasted_iota(jnp.int32, z.shape, 1)
    lf = jnp.where(lane < H_FOX, _log_sigmoid(z), 0.0)
    if not transposed:
        lf_ref[...] = lf
        return
    @pl.when(pl.program_id(1) == 0)
    def _():
        carry_ref[...] = jnp.zeros_like(carry_ref)
    hi = lf.astype(BF16)
    r1 = lf - hi.astype(F32)
    mid = r1.astype(BF16)
    lo = (r1 - mid.astype(F32)).astype(BF16)
    tri = tri_ref[...]
    cum = _nn(tri, hi) + _nn(tri, mid) + _nn(tri, lo) + carry_ref[...]
    carry_ref[...] = cum[tm - 1:tm, :]
    cumT_ref[...] = cum.T[0:SUBLANES, :]
    lfT_ref[...] = lf.T[0:SUBLANES, :]
    ones_lanes = ((lane >= BIAS_LANE + 3) & (lane < BIAS_LANE + 6)).astype(F32)
    for hd in range(H_FOX):
        grp = pk[:, (hd // 2) * LANES:(hd // 2 + 1) * LANES]
        if hd % 2:
            grp = pltpu.roll(grp, HEAD_DIM, 1)
        c_hi, c_mid, c_lo = _three_bf16(jnp.broadcast_to(cum[:, hd:hd + 1], grp.shape))
        bias = jnp.where(lane == BIAS_LANE, c_hi,
                         jnp.where(lane == BIAS_LANE + 1, c_mid, jnp.where(lane == BIAS_LANE + 2, c_lo, ones_lanes)))
        k_ref[:, hd * LANES:(hd + 1) * LANES] = jnp.where(lane < HEAD_DIM, grp, bias).astype(BF16)


def _inproj_prompt(x, sc, sh, w1, bfg, tri, prev):
    b, s, d = x.shape
    conv_ch = d - QK_W
    tm = min(ROW_TILE, s)
    ta = min(ATTN_TILE, s)
    nb = s // ta
    n_prev = 0 if prev is None else prev[0].shape[0]
    row = lambda w: pl.BlockSpec((None, tm, w), lambda i, j: (i, j, 0))
    colT = lambda r: pl.BlockSpec((None, r, tm), lambda i, j: (i, 0, j))
    stack = lambda n, r: pl.BlockSpec((n, None, r, tm), lambda i, j: (0, i, 0, j))
    vec = pl.BlockSpec((None, 1, d), lambda i, j: (i, 0, 0))
    state_widths = (FOX_W, FOX_W, DIFF_W, DIFF_W)
    out_shape = (
        jax.ShapeDtypeStruct((b, QK_W, s), BF16),
        jax.ShapeDtypeStruct((b, s, KROW_W), BF16),
        jax.ShapeDtypeStruct((b, nb, QK_W, ta), BF16),
    ) + tuple(jax.ShapeDtypeStruct((n_prev + 1, b, r, s), F32) for r in state_widths) + (
        jax.ShapeDtypeStruct((b, s, conv_ch), F32),
        jax.ShapeDtypeStruct((b, SUBLANES, s), F32),
        jax.ShapeDtypeStruct((b, SUBLANES, s), F32),
    )
    out_specs = (colT(QK_W), row(KROW_W),
                 pl.BlockSpec((None, tm // ta, QK_W, ta), lambda i, j: (i, j, 0, 0)),
                 ) + tuple(stack(n_prev + 1, r) for r in state_widths) + (row(conv_ch),
                 colT(SUBLANES), colT(SUBLANES))
    prev = () if prev is None else tuple(prev)
    return pl.pallas_call(
        functools.partial(_inproj_kernel, transposed=True, conv_ch=conv_ch, attn_tile=ta, n_prev=n_prev),
        out_shape=out_shape,
        grid=(b, s // tm),
        in_specs=[row(d), vec, vec, _const_spec(w1.shape), _const_spec(bfg.shape), _const_spec(tri.shape)]
                 + [stack(n_prev, r) for r, _ in zip(state_widths, prev)],
        out_specs=out_specs,
        scratch_shapes=[pltpu.VMEM((1, LANES), F32)],
        compiler_params=pltpu.CompilerParams(dimension_semantics=("arbitrary", "arbitrary"),
                                             vmem_limit_bytes=VMEM_LIMIT),
        name="inproj_prompt",
    )(x, sc, sh, w1, bfg, tri, *prev)


def _inproj_sample(x, sc, sh, w1, bfg):
    n, d = x.shape
    conv_ch = d - QK_W
    out_shape = (jax.ShapeDtypeStruct((n, QK_W), F32), jax.ShapeDtypeStruct((n, QK_W), F32),
                 jax.ShapeDtypeStruct((n, QK_W), F32), jax.ShapeDtypeStruct((n, conv_ch), F32),
                 jax.ShapeDtypeStruct((n, LANES), F32))
    return pl.pallas_call(
        functools.partial(_inproj_kernel, transposed=False, conv_ch=conv_ch, attn_tile=0),
        out_shape=out_shape,
        compiler_params=pltpu.CompilerParams(vmem_limit_bytes=VMEM_LIMIT),
        name="inproj_sample",
    )(x, sc, sh, w1, bfg)


N_GROUPS = QK_W // LANES
N_FOX_GROUPS = FOX_W // LANES
MAPS_PER_FOX_GROUP = LANES // HEAD_DIM
MAPS_PER_DIFF_GROUP = LANES // D_DIFF
N_MAPS = N_FOX_GROUPS * MAPS_PER_FOX_GROUP + (N_GROUPS - N_FOX_GROUPS) * MAPS_PER_DIFF_GROUP
LOG2E = math.log2(math.e)
SCORE_LOOKAHEAD = 4


def _group_maps(g):
    if g < N_FOX_GROUPS:
        return g * MAPS_PER_FOX_GROUP, MAPS_PER_FOX_GROUP, HEAD_DIM
    first = N_FOX_GROUPS * MAPS_PER_FOX_GROUP + (g - N_FOX_GROUPS) * MAPS_PER_DIFF_GROUP
    return first, MAPS_PER_DIFF_GROUP, D_DIFF


def _prompt_attn_kernel(qT_ref, k_ref, vT_ref, cumT_ref, lam_ref, g_ref, o_ref,
                        qm_scr, m_scr, l_scr, acc_scr, *, lam_init):
    i = pl.program_id(1)
    t = qT_ref.shape[1]
    row128 = lax.broadcasted_iota(jnp.int32, (LANES, t), 0)
    row64 = lax.broadcasted_iota(jnp.int32, (HEAD_DIM, t), 0)

    for hd in range(H_FOX):
        r0 = (hd // 2) * LANES + (hd % 2) * HEAD_DIM
        c_hi, c_mid, c_lo = _three_bf16(cumT_ref[hd:hd + 1, :])
        bias_rows = jnp.where(row64 < 3, -1.0,
                              jnp.where(row64 == 3, c_hi, jnp.where(row64 == 4, c_mid,
                                                                    jnp.where(row64 == 5, c_lo, 0.0))))
        qm_scr[hd, 0:HEAD_DIM, :] = qT_ref[r0:r0 + HEAD_DIM, :]
        qm_scr[hd, HEAD_DIM:LANES, :] = bias_rows.astype(BF16)
    for g in range(N_FOX_GROUPS, N_GROUPS):
        first, n, width = _group_maps(g)
        qT = qT_ref[g * LANES:(g + 1) * LANES, :].astype(F32)
        for mp in range(n):
            qm_scr[first + mp] = jnp.where((row128 // width) == mp, qT, 0.0).astype(BF16)
    m_scr[...] = jnp.full_like(m_scr, NEG_INF)
    l_scr[...] = jnp.zeros_like(l_scr)
    acc_scr[...] = jnp.zeros_like(acc_scr)

    map_group = [g for g in range(N_GROUPS) for _ in range(_group_maps(g)[1])]

    def step(j, masked, nblk=1):
        start = pl.multiple_of(j * t, t)
        k_tiles, vT_tiles = {}, {}
        if masked:
            causal = lax.broadcasted_iota(jnp.int32, (t, t), 0) <= lax.broadcasted_iota(jnp.int32, (t, t), 1)

        def score(mi):
            kg = mi if mi < H_FOX else H_FOX + map_group[mi] - N_FOX_GROUPS
            if kg not in k_tiles:
                k_tiles[kg] = k_ref[pl.ds(start, nblk * t), kg * LANES:(kg + 1) * LANES]
            return _nn(k_tiles[kg], qm_scr[mi])

        def update(mi, s):
            g = map_group[mi]
            if g not in vT_tiles:
                vT_tiles[g] = jnp.concatenate([vT_ref[j + bb, g * LANES:(g + 1) * LANES, :] for bb in range(nblk)],
                                              axis=1)
            c2 = LOG2E if g < N_FOX_GROUPS else LOG2E * D_DIFF ** -0.5
            if masked:
                s = jnp.where(causal, s, NEG_INF)
            m_prev = m_scr[mi]
            m_new = jnp.maximum(m_prev, jnp.max(s, axis=0, keepdims=True))
            alpha = jnp.exp2((m_prev - m_new) * c2)
            p = jnp.exp2((s - m_new) * c2)
            l_scr[mi] = alpha * l_scr[mi] + jnp.sum(p, axis=0, keepdims=True)
            acc_scr[mi] = alpha * acc_scr[mi] + _nn(vT_tiles[g], p.astype(BF16))
            m_scr[mi] = m_new

        pending = []
        for idx in range(N_MAPS + SCORE_LOOKAHEAD):
            if idx < N_MAPS:
                pending.append((idx, score(idx)))
            if idx >= SCORE_LOOKAHEAD:
                update(*pending.pop(0))

    def body(u, carry):
        step(2 * u, False, 2)
        return carry

    lax.fori_loop(0, i // 2, body, 0)

    @pl.when(i % 2 == 1)
    def _():
        step(i - 1, False)

    step(i, True)

    def out_T(mi):
        return acc_scr[mi] / l_scr[mi]

    lam = _lambda(lam_ref, lam_init)
    low = row128 < HEAD_DIM
    for g in range(N_GROUPS):
        first, n, _ = _group_maps(g)
        cols = slice(g * LANES, (g + 1) * LANES)
        if g < N_FOX_GROUPS:
            o_ref[:, cols] = jnp.where(low, out_T(first), out_T(first + 1)).T.astype(o_ref.dtype)
        else:
            dT = jnp.where(low, out_T(first) - lam * out_T(first + 1), out_T(first + 2) - lam * out_T(first + 3))
            sq = dT * dT
            ms0 = jnp.mean(sq[0:HEAD_DIM], axis=0, keepdims=True)
            ms1 = jnp.mean(sq[HEAD_DIM:LANES], axis=0, keepdims=True)
            ms = jnp.where(low, ms0, ms1)
            dn = (dT * lax.rsqrt(ms + LN_EPS)).T * g_ref[...] * (1.0 - lam_init)
            o_ref[:, cols] = dn.astype(o_ref.dtype)


def _prompt_attn(qT, k, vT, cumT, lam4, g128, lam_init):
    b, s, _ = k.shape
    t = vT.shape[-1]
    return pl.pallas_call(
        functools.partial(_prompt_attn_kernel, lam_init=lam_init),
        out_shape=jax.ShapeDtypeStruct((b, s, QK_W), BF16),
        grid=(b, s // t),
        in_specs=[pl.BlockSpec((None, QK_W, t), lambda bi, i: (bi, 0, i)),
                  pl.BlockSpec((None, s, KROW_W), lambda bi, i: (bi, 0, 0)),
                  pl.BlockSpec((None, s // t, QK_W, t), lambda bi, i: (bi, 0, 0, 0)),
                  pl.BlockSpec((None, SUBLANES, t), lambda bi, i: (bi, 0, i)),
                  _const_spec(lam4.shape), _const_spec(g128.shape)],
        out_specs=pl.BlockSpec((None, t, QK_W), lambda bi, i: (bi, i, 0)),
        scratch_shapes=[pltpu.VMEM((N_MAPS, LANES, t), BF16),
                        pltpu.VMEM((N_MAPS, 1, t), F32), pltpu.VMEM((N_MAPS, 1, t), F32),
                        pltpu.VMEM((N_MAPS, LANES, t), F32)],
        compiler_params=pltpu.CompilerParams(dimension_semantics=("arbitrary", "arbitrary"),
                                             vmem_limit_bytes=VMEM_LIMIT),
        name="prompt_attn",
    )(qT, k, vT, cumT, lam4, g128)


HALO = 32


def _conv_finish(y, cb_ref, g_ref, b_ref):
    y = _layernorm(y + cb_ref[...], g_ref[...], b_ref[...])
    return y * _sigmoid(y)


def _conv_prompt_kernel(u_ref, halo_ref, cw_ref, cb_ref, g_ref, b_ref, o_ref, win_scr, shift_scr):
    tm = u_ref.shape[0]
    j = pl.program_id(1)
    halo = halo_ref[...]
    win_scr[0:HALO, :] = jnp.where(j == 0, jnp.zeros_like(halo), halo)
    win_scr[HALO:HALO + tm, :] = u_ref[...]
    base = HALO - (CONV_WIDTH - 1)
    acc = None
    for phase in range(SUBLANES):
        taps = [w for w in range(CONV_WIDTH) if (base + w) % SUBLANES == phase]
        span = base + taps[-1] - phase + tm
        shift_scr[0:span, :] = win_scr[phase:phase + span, :]
        for w in taps:
            o = base + w - phase
            term = shift_scr[o:o + tm, :] * cw_ref[w:w + 1, :]
            acc = term if acc is None else acc + term
    o_ref[...] = _conv_finish(acc, cb_ref, g_ref, b_ref).astype(o_ref.dtype)


def _conv_prompt(u, cw, cb, g, bb):
    b, s, c = u.shape
    tm = min(ROW_TILE, s)
    hb = tm // HALO
    return pl.pallas_call(
        _conv_prompt_kernel,
        out_shape=jax.ShapeDtypeStruct((b, s, c), BF16),
        grid=(b, s // tm),
        in_specs=[pl.BlockSpec((None, tm, c), lambda i, j: (i, j, 0)),
                  pl.BlockSpec((None, HALO, c), lambda i, j: (i, jnp.maximum(j * hb - 1, 0), 0)),
                  _const_spec(cw.shape), _const_spec(cb.shape), _const_spec(g.shape), _const_spec(bb.shape)],
        out_specs=pl.BlockSpec((None, tm, c), lambda i, j: (i, j, 0)),
        scratch_shapes=[pltpu.VMEM((HALO + tm, c), F32), pltpu.VMEM((HALO + tm, c), F32)],
        compiler_params=pltpu.CompilerParams(dimension_semantics=("arbitrary", "arbitrary"),
                                             vmem_limit_bytes=VMEM_LIMIT),
        name="conv_prompt",
    )(u, u, cw, cb, g, bb)


def _conv_sample_kernel(buf_ref, u_ref, cw_ref, cb_ref, g_ref, b_ref, o_ref):
    acc = u_ref[...] * cw_ref[CONV_WIDTH - 1:CONV_WIDTH, :]
    for w in range(CONV_WIDTH - 1):
        acc = acc + buf_ref[w] * cw_ref[w:w + 1, :]
    o_ref[...] = _conv_finish(acc, cb_ref, g_ref, b_ref).astype(o_ref.dtype)


def _conv_sample(buf_t, u, cw, cb, g, bb):
    return pl.pallas_call(
        _conv_sample_kernel,
        out_shape=jax.ShapeDtypeStruct(u.shape, BF16),
        name="conv_sample",
    )(buf_t, u, cw, cb, g, bb)


def _post_kernel(x_ref, ao_ref, co_ref, ga_ref, shf_ref, scf_ref, gf_ref, woa_ref, woc_ref,
                 g1_ref, b1_ref, win_ref, wout_ref, g2_ref, b2_ref, o_ref, a_scr, *, alpha, d_ff):
    y = _nn(ao_ref[...], woa_ref[...]) + _nn(co_ref[...], woc_ref[...])
    x1 = _layernorm(alpha * x_ref[...] + (1.0 + ga_ref[...]) * y, g1_ref[...], b1_ref[...])
    h = (x1 * (1.0 + scf_ref[...]) + shf_ref[...]).astype(BF16)
    for c in range(d_ff // FFN_CHUNK):
        lo = c * FFN_CHUNK
        gate = _nn(h, win_ref[:, lo:lo + FFN_CHUNK])
        up = _nn(h, win_ref[:, d_ff + lo:d_ff + lo + FFN_CHUNK])
        a_scr[:, lo:lo + FFN_CHUNK] = (gate * _sigmoid(gate) * up).astype(BF16)
    y2 = _nn(a_scr[...], wout_ref[...])
    o_ref[...] = _layernorm(alpha * x1 + (1.0 + gf_ref[...]) * y2, g2_ref[...], b2_ref[...])


def _post(x, ao, co, mods, woa, woc, g1, b1, w_in, w_out, g2, b2, alpha):
    gdim, r, d = x.shape
    d_ff = w_out.shape[0]
    tm = min(ROW_TILE, r)
    row = lambda w: pl.BlockSpec((None, tm, w), lambda i, j: (i, j, 0))

    def mod_spec(m):
        if m.shape[1] == 1:
            return pl.BlockSpec((None, 1, d), lambda i, j: (i, 0, 0))
        return row(d)

    consts = (woa, woc, g1, b1, w_in, w_out, g2, b2)
    return pl.pallas_call(
        functools.partial(_post_kernel, alpha=alpha, d_ff=d_ff),
        out_shape=jax.ShapeDtypeStruct(x.shape, F32),
        grid=(gdim, r // tm),
        in_specs=[row(d), row(ao.shape[-1]), row(co.shape[-1])] + [mod_spec(m) for m in mods]
                 + [_const_spec(c.shape) for c in consts],
        out_specs=row(d),
        scratch_shapes=[pltpu.VMEM((tm, d_ff), BF16)],
        compiler_params=pltpu.CompilerParams(dimension_semantics=("arbitrary", "arbitrary"),
                                             vmem_limit_bytes=VMEM_LIMIT),
        name="post",
    )(x, ao, co, *mods, *consts)


def _decode_ops(pt_ref, q_ref, kn_ref, vn_ref, lfn_ref, lam_ref, g_ref, utri_ref,
                fk_hbm, fv_hbm, lf_hbm, dk_hbm, dv_hbm, o_ref,
                fk_buf, fv_buf, lf_buf, dk_buf, dv_buf, sem,
                mf_scr, lsf_scr, accf_scr, md_scr, lsd_scr, accd_scr, carry_scr,
                *, layer, n_pages, lam_init):
    npg = PAGES_PER_STEP
    n_chunks = n_pages // npg
    hbm = (fk_hbm, fv_hbm, lf_hbm, dk_hbm, dv_hbm)
    bufs = (fk_buf, fv_buf, lf_buf, dk_buf, dv_buf)

    def copies(ch, sl):
        bb = ch // n_chunks
        cc = ch % n_chunks
        first = bb * n_pages + (n_chunks - 1 - cc) * npg
        out = []
        for p in range(npg):
            page = pt_ref[first + p]
            for a in range(5):
                out.append(pltpu.make_async_copy(hbm[a].at[layer, page], bufs[a].at[sl, p], sem.at[sl, a]))
        return out

    def start(ch, sl):
        for cp in copies(ch, sl):
            cp.start()

    row_f = lax.broadcasted_iota(jnp.int32, (SUBLANES, FOX_W), 0)
    lane_f = lax.broadcasted_iota(jnp.int32, (SUBLANES, FOX_W), 1)
    own_f = (lane_f // HEAD_DIM) == row_f
    row_d = lax.broadcasted_iota(jnp.int32, (SUBLANES, DIFF_W), 0)
    lane_d = lax.broadcasted_iota(jnp.int32, (SUBLANES, DIFF_W), 1)
    own_dq = (lane_d // D_DIFF) == row_d
    own_dv = (lane_d // HEAD_DIM) == (row_d // 2)

    q = q_ref[...]
    qf = jnp.where(own_f, q[:, 0:FOX_W], 0.0).astype(BF16)
    qd = jnp.where(own_dq, q[:, FOX_W:QK_W], 0.0).astype(BF16)
    d_scale = D_DIFF ** -0.5

    def begin():
        kn = kn_ref[...].astype(BF16).astype(F32)
        vn = vn_ref[...].astype(BF16).astype(F32)
        mf_scr[...] = jnp.sum(qf.astype(F32) * kn[:, 0:FOX_W], axis=-1, keepdims=True)
        md_scr[...] = jnp.sum(qd.astype(F32) * kn[:, FOX_W:QK_W], axis=-1, keepdims=True) * d_scale
        lsf_scr[...] = jnp.ones_like(lsf_scr)
        lsd_scr[...] = jnp.ones_like(lsd_scr)
        accf_scr[...] = jnp.broadcast_to(vn[:, 0:FOX_W], accf_scr.shape)
        accd_scr[...] = jnp.broadcast_to(vn[:, FOX_W:QK_W], accd_scr.shape)
        carry_scr[...] = lfn_ref[...]

    def online(slot, s_list, m_scr, l_scr, acc_scr, v_buf):
        m_prev = m_scr[...]
        m_new = m_prev
        for s in s_list:
            m_new = jnp.maximum(m_new, jnp.max(s, axis=-1, keepdims=True))
        alpha = jnp.exp(m_prev - m_new)
        l_new = alpha * l_scr[...]
        acc = alpha * acc_scr[...]
        for p, s in enumerate(s_list):
            e = jnp.exp(s - m_new)
            l_new = l_new + jnp.sum(e, axis=-1, keepdims=True)
            acc = acc + _nt(e.astype(BF16), v_buf[slot, p].astype(BF16))
        m_scr[...] = m_new
        l_scr[...] = l_new
        acc_scr[...] = acc

    def chunk(ch, slot):
        for cp in copies(ch, slot):
            cp.wait()
        utri = utri_ref[...]
        carry = carry_scr[...]
        bias = [None] * npg
        for p in reversed(range(npg)):
            lf = lf_buf[slot, p]
            hi = lf.astype(BF16)
            r1 = lf - hi.astype(F32)
            mid = r1.astype(BF16)
            lo = (r1 - mid.astype(F32)).astype(BF16)
            bias[p] = _nn(hi, utri) + _nn(mid, utri) + _nn(lo, utri) + carry
            carry = carry + jnp.sum(lf, axis=-1, keepdims=True)
        carry_scr[...] = carry
        sf = [_nn(qf, fk_buf[slot, p].astype(BF16)) + bias[p] for p in range(npg)]
        sd = [_nn(qd, dk_buf[slot, p].astype(BF16)) * d_scale for p in range(npg)]
        online(slot, sf, mf_scr, lsf_scr, accf_scr, fv_buf)
        online(slot, sd, md_scr, lsd_scr, accd_scr, dv_buf)

    def finish():
        zf = accf_scr[...] / lsf_scr[...]
        fox = jnp.sum(jnp.where(own_f, zf, 0.0), axis=0, keepdims=True)
        lam = _lambda(lam_ref, lam_init)
        coef = jnp.where((row_d % 2) == 0, 1.0, -lam)
        zd = coef * (accd_scr[...] / lsd_scr[...])
        dif = jnp.sum(jnp.where(own_dv, zd, 0.0), axis=0, keepdims=True)
        sq = dif * dif
        lane1 = lax.broadcasted_iota(jnp.int32, (1, DIFF_W), 1)
        ms = jnp.zeros_like(dif)
        for hd in range(H_DIFF):
            in_head = (lane1 // HEAD_DIM) == hd
            ms_h = jnp.sum(jnp.where(in_head, sq, 0.0), axis=-1, keepdims=True) / HEAD_DIM
            ms = jnp.where(in_head, ms_h, ms)
        dn = dif * lax.rsqrt(ms + LN_EPS) * g_ref[...] * (1.0 - lam_init)
        o_ref[:, 0:FOX_W] = fox.astype(o_ref.dtype)
        o_ref[:, FOX_W:QK_W] = dn.astype(o_ref.dtype)

    return start, begin, chunk, finish


def _decode_kernel(pt_ref, *refs, layer, n_pages, lam_init):
    start, begin, chunk, finish = _decode_ops(pt_ref, *refs, layer=layer, n_pages=n_pages, lam_init=lam_init)
    c = pl.program_id(1)
    n_chunks = pl.num_programs(1)
    ch = pl.program_id(0) * n_chunks + c
    slot = ch % 2

    @pl.when(ch == 0)
    def _():
        start(ch, slot)

    @pl.when(ch + 1 < pl.num_programs(0) * n_chunks)
    def _():
        start(ch + 1, 1 - slot)

    @pl.when(c == 0)
    def _():
        begin()

    chunk(ch, slot)

    @pl.when(c == n_chunks - 1)
    def _():
        finish()


def _decode_attn(page_table, q, kn, vn, lfn, lam4, g256, utri, caches, layer, lam_init):
    nb, n_pages = page_table.shape
    npg = PAGES_PER_STEP
    page = caches[0].shape[-1]
    seq = lambda *blk: pl.BlockSpec((None,) + blk, lambda b, c, pt: (b, 0, 0))
    const = lambda a: pl.BlockSpec(a.shape, lambda b, c, pt: (0,) * a.ndim)
    any_spec = pl.BlockSpec(memory_space=pl.ANY)
    grid_spec = pltpu.PrefetchScalarGridSpec(
        num_scalar_prefetch=1,
        grid=(nb, n_pages // npg),
        in_specs=[seq(1, QK_W), seq(1, QK_W), seq(1, QK_W), seq(SUBLANES, LANES), const(lam4), const(g256),
                  const(utri)] + [any_spec] * 5,
        out_specs=seq(1, QK_W),
        scratch_shapes=[pltpu.VMEM((2, npg, FOX_W, page), F32), pltpu.VMEM((2, npg, FOX_W, page), F32),
                        pltpu.VMEM((2, npg, SUBLANES, page), F32),
                        pltpu.VMEM((2, npg, DIFF_W, page), F32), pltpu.VMEM((2, npg, DIFF_W, page), F32),
                        pltpu.SemaphoreType.DMA((2, 5)),
                        pltpu.VMEM((SUBLANES, 1), F32), pltpu.VMEM((SUBLANES, 1), F32),
                        pltpu.VMEM((SUBLANES, FOX_W), F32),
                        pltpu.VMEM((SUBLANES, 1), F32), pltpu.VMEM((SUBLANES, 1), F32),
                        pltpu.VMEM((SUBLANES, DIFF_W), F32),
                        pltpu.VMEM((SUBLANES, LANES), F32)],
    )
    return pl.pallas_call(
        functools.partial(_decode_kernel, layer=layer, n_pages=n_pages, lam_init=lam_init),
        out_shape=jax.ShapeDtypeStruct((nb, 1, QK_W), BF16),
        grid_spec=grid_spec,
        compiler_params=pltpu.CompilerParams(dimension_semantics=("arbitrary", "arbitrary"),
                                             vmem_limit_bytes=VMEM_LIMIT),
        name="decode_attn",
    )(page_table.reshape(-1), q, kn, vn, lfn, lam4, g256, utri, *caches)


def _pack_w_in(w):
    o = 0
    parts = {}
    for name, width in (("fq", FOX_W), ("fk", FOX_W), ("fv", FOX_W), ("fg", H_FOX), ("dq", DIFF_W),
                        ("dk", DIFF_W), ("dv", DIFF_W)):
        parts[name] = w[:, o:o + width]
        o += width
    conv = w[:, o:]
    fg = jnp.pad(parts["fg"], ((0, 0), (0, LANES - H_FOX)))
    packed = jnp.concatenate([parts["fq"] * (HEAD_DIM ** -0.5), parts["dq"], parts["fk"], parts["dk"],
                              parts["fv"], parts["dv"], conv, fg], axis=1)
    return packed.astype(BF16)


def kernel(x_prompt, x_sample, cache_fox_k, cache_fox_v, cache_fox_logf, cache_diff_k, cache_diff_v, state_conv, page_table, c_prompt, c_sample, w_ada, b_ada, w_in, b_fgate, lambda_q1, lambda_k1, lambda_q2, lambda_k2, diff_norm_g, conv_w, conv_b, conv_ln_g, conv_ln_b, w_o, ln_mix_g, ln_mix_b, w_ffn_in, w_ffn_out, ln_ffn_g, ln_ffn_b):
    depth = w_in.shape[0]
    b, s, d = x_prompt.shape
    nb = x_sample.shape[0]
    conv_ch = d - QK_W
    n_pool, page = cache_fox_k.shape[1:3]
    alpha = (2.0 * depth) ** 0.25

    fkT = jnp.transpose(cache_fox_k, (0, 1, 3, 4, 2)).reshape(depth, n_pool, FOX_W, page)
    fvT = jnp.transpose(cache_fox_v, (0, 1, 3, 4, 2)).reshape(depth, n_pool, FOX_W, page)
    dkT = jnp.transpose(cache_diff_k, (0, 1, 3, 4, 2)).reshape(depth, n_pool, DIFF_W, page)
    dvT = jnp.transpose(cache_diff_v, (0, 1, 3, 4, 2)).reshape(depth, n_pool, DIFF_W, page)
    lfT = jnp.pad(jnp.transpose(cache_fox_logf, (0, 1, 3, 2)), ((0, 0), (0, 0), (0, SUBLANES - H_FOX), (0, 0)))
    conv_hist = jnp.transpose(state_conv, (0, 2, 1, 3))

    ada = _ada(jnp.concatenate([c_prompt, c_sample], axis=0), w_ada, b_ada)

    tm = min(ROW_TILE, s)
    tri = jnp.tril(jnp.ones((tm, tm), F32)).astype(BF16)
    utri = jnp.triu(jnp.ones((page, page), F32), 1).T.astype(BF16)
    xp, xs = x_prompt, x_sample.reshape(nb, d)
    state = None
    new_p = [[] for _ in range(6)]
    new_s = [[] for _ in range(6)]
    for l in range(depth):
        lam_init = 0.8 - 0.6 * math.exp(-0.3 * l)
        w1 = _pack_w_in(w_in[l])
        bfg = jnp.pad(b_fgate[l], (0, LANES - H_FOX)).reshape(1, LANES)
        lam4 = jnp.stack([lambda_q1[l], lambda_k1[l], lambda_q2[l], lambda_k2[l]])
        g128 = jnp.tile(diff_norm_g[l], 2).reshape(1, LANES)
        g256 = jnp.tile(diff_norm_g[l], H_DIFF).reshape(1, DIFF_W)
        cw, cb = conv_w[l], conv_b[l].reshape(1, conv_ch)
        cg, cbb = conv_ln_g[l].reshape(1, conv_ch), conv_ln_b[l].reshape(1, conv_ch)
        wo = w_o[l].astype(BF16)
        woa, woc = wo[0:QK_W], wo[QK_W:]
        wfi, wfo = w_ffn_in[l].astype(BF16), w_ffn_out[l].astype(BF16)
        g1, b1 = ln_mix_g[l].reshape(1, d), ln_mix_b[l].reshape(1, d)
        g2, b2 = ln_ffn_g[l].reshape(1, d), ln_ffn_b[l].reshape(1, d)
        mods = [ada[l, :, i * d:(i + 1) * d] for i in range(6)]
        mp = [m[0:b].reshape(b, 1, d) for m in mods]
        ms = [m[b:b + nb].reshape(1, nb, d) for m in mods]

        (qT, kr, vT, *state, u, lf_t, cumT) = _inproj_prompt(xp, mp[1], mp[0], w1, bfg, tri, state)
        qs, ks, vs, us, lfs = _inproj_sample(xs, ms[1][0], ms[0][0], w1, bfg)
        ao = _prompt_attn(qT, kr, vT, cumT, lam4, g128, lam_init)
        co = _conv_prompt(u, cw, cb, cg, cbb)
        lfn = jnp.broadcast_to(jnp.pad(lfs[:, 0:H_FOX], ((0, 0), (0, SUBLANES - H_FOX)))[:, :, None],
                               (nb, SUBLANES, LANES))
        r3 = lambda a: a.reshape(nb, 1, a.shape[-1])
        xp = _post(xp, ao, co, (mp[2], mp[3], mp[4], mp[5]), woa, woc, g1, b1, wfi, wfo, g2, b2, alpha)
        aos = _decode_attn(page_table, r3(qs), r3(ks), r3(vs), lfn, lam4, g256, utri,
                           (fkT, fvT, lfT, dkT, dvT), l, lam_init)
        new_p[2].append(jnp.transpose(lf_t[:, 0:H_FOX, :], (0, 2, 1)))
        new_p[5].append(u[:, s - (CONV_WIDTH - 1):, :])

        cos = _conv_sample(conv_hist[l], us, cw, cb, cg, cbb)
        xs = _post(xs.reshape(1, nb, d), aos.reshape(1, nb, QK_W), cos.reshape(1, nb, conv_ch),
                   (ms[2], ms[3], ms[4], ms[5]), woa, woc, g1, b1, wfi, wfo, g2, b2, alpha).reshape(nb, d)
        new_s[0].append(ks[:, 0:FOX_W].reshape(nb, 1, H_FOX, HEAD_DIM))
        new_s[1].append(vs[:, 0:FOX_W].reshape(nb, 1, H_FOX, HEAD_DIM))
        new_s[2].append(lfs[:, 0:H_FOX].reshape(nb, 1, H_FOX))
        new_s[3].append(ks[:, FOX_W:QK_W].reshape(nb, 1, H_DIFF, HEAD_DIM))
        new_s[4].append(vs[:, FOX_W:QK_W].reshape(nb, 1, H_DIFF, HEAD_DIM))
        new_s[5].append(jnp.concatenate([state_conv[l][:, 1:, :], us[:, None, :]], axis=1))

    unT = lambda a, h: jnp.transpose(a.reshape(depth, b, h, HEAD_DIM, s), (0, 1, 4, 2, 3))
    fk_p, fv_p, dk_p, dv_p = state
    return (xp, xs.reshape(nb, 1, d),
            unT(fk_p, H_FOX), unT(fv_p, H_FOX), jnp.stack(new_p[2]),
            unT(dk_p, H_DIFF), unT(dv_p, H_DIFF), jnp.stack(new_p[5]),
            jnp.stack(new_s[0]), jnp.stack(new_s[1]), jnp.stack(new_s[2]),
            jnp.stack(new_s[3]), jnp.stack(new_s[4]), jnp.stack(new_s[5]))
```
